```python
import jax, jax.numpy as jnp
from jax import lax
import numpy as np

D_MODEL = 1024
BATCH = 16
SEQ = 4096
DEPTH = 1

MLSTM_HEADS = 4
MLSTM_HEAD_DIM = 256
MLSTM_WIDTH = MLSTM_HEADS * MLSTM_HEAD_DIM
MLSTM_CHUNK = 64
CONV_WIDTH = 4
ATTN_GROUPS = ((128, 1), (512, 4), (2048, 16))
N_GROUPS = len(ATTN_GROUPS)
HEADS_PER_GROUP = 4
ATTN_HEAD_DIM = 128
ATTN_HEADS = N_GROUPS * HEADS_PER_GROUP
ATTN_QKV_WIDTH = ATTN_HEADS * ATTN_HEAD_DIM
ATTN_OUT_WIDTH = HEADS_PER_GROUP * ATTN_HEAD_DIM
ROPE_DIM = ATTN_HEAD_DIM // 4
ROPE_THETA = 500000.0
EPS = 1e-6
IN_SPLITS = (MLSTM_WIDTH, MLSTM_WIDTH, MLSTM_WIDTH, MLSTM_WIDTH, MLSTM_HEADS, MLSTM_HEADS,
             ATTN_QKV_WIDTH, ATTN_QKV_WIDTH, ATTN_QKV_WIDTH, ATTN_OUT_WIDTH, D_MODEL, D_MODEL)
IN_WIDTH = sum(IN_SPLITS)

kernel_name = 'hybrid_mlstm_dilated_attn_block'


def rms_norm(x, gain):
    xf = x.astype(jnp.float32)
    y = xf * lax.rsqrt(jnp.mean(xf * xf, axis=-1, keepdims=True) + EPS)
    return (y * gain.astype(jnp.float32)).astype(x.dtype)


def causal_depthwise_conv(x, w, b):
    y = lax.conv_general_dilated(x, w[:, None, :], window_strides=(1,),
                                 padding=((CONV_WIDTH - 1, 0),),
                                 dimension_numbers=('NWC', 'WIO', 'NWC'),
                                 feature_group_count=x.shape[-1])
    return y + b


def partial_rotary(t, positions):
    half = ROPE_DIM // 2
    inv_freq = ROPE_THETA ** (-jnp.arange(half, dtype=jnp.float32) / half)
    ang = positions.astype(jnp.float32)[..., None] * inv_freq
    cos = jnp.cos(ang)[:, :, None, :]
    sin = jnp.sin(ang)[:, :, None, :]
    tr = t[..., :ROPE_DIM].astype(jnp.float32)
    x1, x2 = tr[..., :half], tr[..., half:]
    rot = jnp.concatenate([x1 * cos - x2 * sin, x2 * cos + x1 * sin], axis=-1)
    return jnp.concatenate([rot.astype(t.dtype), t[..., ROPE_DIM:]], axis=-1)


def mlstm_chunkwise(q, k, v, i_pre, f_pre):
    B, S, H, Dh = q.shape
    L = MLSTM_CHUNK
    NC = S // L
    f32 = jnp.float32

    def to_chunks(t):
        t = t.reshape((B, NC, L, H) + t.shape[3:])
        return jnp.moveaxis(t, (1, 3), (0, 2))

    qc = to_chunks(q.astype(f32))
    kc = to_chunks(k.astype(f32))
    vc = to_chunks(v.astype(f32))
    igc = to_chunks(i_pre.astype(f32))
    ac = jnp.cumsum(to_chunks(jax.nn.log_sigmoid(f_pre.astype(f32))), axis=-1)
    causal = jnp.tril(jnp.ones((L, L), dtype=bool))

    def step(carry, inp):
        C, n, m = carry
        qj, kj, vj, aj, ij = inp
        g = aj[..., -1]
        dmat = aj[..., :, None] - aj[..., None, :] + ij[..., None, :]
        dmat = jnp.where(causal, dmat, -jnp.inf)
        inter = aj + m[..., None]
        m_row = jnp.maximum(inter, jnp.max(dmat, axis=-1))
        w_intra = jnp.exp(dmat - m_row[..., None])
        w_inter = jnp.exp(inter - m_row)
        s = jnp.einsum('bhld,bhsd->bhls', qj, kj) * w_intra
        num = jnp.einsum('bhls,bhse->bhle', s, vj) + w_inter[..., None] * jnp.einsum('bhld,bhde->bhle', qj, C)
        den = jnp.sum(s, axis=-1) + w_inter * jnp.einsum('bhld,bhd->bhl', qj, n)
        h = num / jnp.maximum(jnp.abs(den), jnp.exp(-m_row))[..., None]
        w_state = g[..., None] - aj + ij
        m_new = jnp.maximum(g + m, jnp.max(w_state, axis=-1))
        decay = jnp.exp(g + m - m_new)
        kw = jnp.exp(w_state - m_new[..., None])[..., None] * kj
        C_new = decay[..., None, None] * C + jnp.einsum('bhld,bhle->bhde', kw, vj)
        n_new = decay[..., None] * n + jnp.sum(kw, axis=2)
        return (C_new, n_new, m_new), h

    init = (jnp.zeros((B, H, Dh, Dh), f32), jnp.zeros((B, H, Dh), f32), jnp.zeros((B, H), f32))
    _, hs = lax.scan(step, init, (qc, kc, vc, ac, igc))
    return hs.transpose(1, 0, 3, 2, 4).reshape(B, S, H, Dh)


def dilated_window_attention(q, k, v, window, dilation):
    B, S, H, Dh = q.shape
    span = window // dilation
    blk = span
    U = S // dilation
    nb = -(-U // blk)
    Up = nb * blk

    def strided(t):
        t = t.reshape(B, U, dilation, H, Dh).transpose(0, 3, 2, 1, 4)
        t = jnp.pad(t, ((0, 0), (0, 0), (0, 0), (0, Up - U), (0, 0)))
        return t.reshape(B, H, dilation, nb, blk, Dh)

    def with_prev(t):
        prev = jnp.pad(t, ((0, 0), (0, 0), (0, 0), (1, 0), (0, 0), (0, 0)))[:, :, :, :nb]
        return jnp.concatenate([prev, t], axis=4)

    qb = strided(q)
    kb = with_prev(strided(k))
    vb = with_prev(strided(v))
    scores = jnp.einsum('bhrnqd,bhrnkd->bhrnqk', qb, kb,
                        preferred_element_type=jnp.float32) * (Dh ** -0.5)
    p = jnp.arange(blk)[:, None]
    sk = jnp.arange(2 * blk)[None, :]
    delta = p + blk - sk
    band = (delta >= 0) & (delta <= span)
    key_u = jnp.arange(nb)[:, None, None] * blk - blk + sk[None]
    valid = band[None] & (key_u >= 0)
    scores = jnp.where(valid, scores, -jnp.inf)
    mx = jnp.max(scores, axis=-1, keepdims=True)
    e = jnp.exp(scores - mx)
    den = jnp.sum(e, axis=-1)
    out = jnp.einsum('bhrnqk,bhrnkd->bhrnqd', e, vb.astype(jnp.float32)) / den[..., None]
    lse = mx[..., 0] + jnp.log(den)
    out = out.reshape(B, H, dilation, Up, Dh)[:, :, :, :U].transpose(0, 3, 2, 1, 4).reshape(B, S, H, Dh)
    lse = lse.reshape(B, H, dilation, Up)[:, :, :, :U].transpose(0, 3, 2, 1).reshape(B, S, H)
    return out, lse


def setup_inputs(seed: int = 0) -> dict:
    key = jax.random.key(seed)
    ks = jax.random.split(key, 20)
    f32 = jnp.float32

    def nrm(k, shape, s):
        return jax.random.normal(k, shape, f32) * s

    x = nrm(ks[0], (BATCH, SEQ, D_MODEL), 1.0)
    c = nrm(ks[1], (BATCH, D_MODEL), 1.0)
    positions = (jax.random.randint(ks[2], (BATCH, 1), 0, 1024) + jnp.arange(SEQ)[None, :]).astype(jnp.int32)
    w_ada = nrm(ks[3], (DEPTH, D_MODEL, 3 * D_MODEL), D_MODEL ** -0.5)
    b_ada = nrm(ks[4], (DEPTH, 3 * D_MODEL), 0.02)
    g_norm = 1.0 + nrm(ks[5], (DEPTH, D_MODEL), 0.02)
    w_in = nrm(ks[6], (DEPTH, D_MODEL, IN_WIDTH), D_MODEL ** -0.5)
    b_igate = nrm(ks[7], (DEPTH, MLSTM_HEADS), 0.1)
    b_fgate = 3.0 + 3.0 * jax.random.uniform(ks[8], (DEPTH, MLSTM_HEADS), f32)
    w_conv = nrm(ks[9], (DEPTH, CONV_WIDTH, MLSTM_WIDTH), CONV_WIDTH ** -0.5)
    b_conv = nrm(ks[10], (DEPTH, MLSTM_WIDTH), 0.02)
    w_q_mlstm = nrm(ks[11], (DEPTH, MLSTM_HEADS, MLSTM_HEAD_DIM, MLSTM_HEAD_DIM), MLSTM_HEAD_DIM ** -0.5)
    w_k_mlstm = nrm(ks[12], (DEPTH, MLSTM_HEADS, MLSTM_HEAD_DIM, MLSTM_HEAD_DIM), MLSTM_HEAD_DIM ** -0.5)
    g_mlstm_head = 1.0 + nrm(ks[13], (DEPTH, MLSTM_HEADS, MLSTM_HEAD_DIM), 0.02)
    skip_mlstm = 1.0 + nrm(ks[14], (DEPTH, MLSTM_WIDTH), 0.02)
    w_out_mlstm = nrm(ks[15], (DEPTH, MLSTM_WIDTH, D_MODEL), MLSTM_WIDTH ** -0.5)
    w_out_attn = nrm(ks[16], (DEPTH, ATTN_OUT_WIDTH, D_MODEL), ATTN_OUT_WIDTH ** -0.5)
    w_out = nrm(ks[17], (DEPTH, D_MODEL, D_MODEL), D_MODEL ** -0.5)
    g_final = 1.0 + nrm(ks[18], (D_MODEL,), 0.02)
    return {'x': x, 'c': c, 'positions': positions, 'w_ada': w_ada, 'b_ada': b_ada,
            'g_norm': g_norm, 'w_in': w_in, 'b_igate': b_igate, 'b_fgate': b_fgate,
            'w_conv': w_conv, 'b_conv': b_conv, 'w_q_mlstm': w_q_mlstm, 'w_k_mlstm': w_k_mlstm,
            'g_mlstm_head': g_mlstm_head, 'skip_mlstm': skip_mlstm, 'w_out_mlstm': w_out_mlstm,
            'w_out_attn': w_out_attn, 'w_out': w_out, 'g_final': g_final}


def reference(x, c, positions, w_ada, b_ada, g_norm, w_in, b_igate, b_fgate, w_conv, b_conv,
              w_q_mlstm, w_k_mlstm, g_mlstm_head, skip_mlstm, w_out_mlstm, w_out_attn, w_out, g_final):
    B, S, _ = x.shape
    f32 = jnp.float32
    split_at = np.cumsum(IN_SPLITS)[:-1].tolist()
    for l in range(DEPTH):
        mod = jax.nn.silu(c) @ w_ada[l] + b_ada[l]
        shift, scale, gate = jnp.split(mod[:, None, :], 3, axis=-1)
        h = rms_norm(x, g_norm[l]) * (1.0 + scale) + shift
        proj = h @ w_in[l]
        xa, va, oa, za, ia, fa, qa, ka, vat, zb, ga, gb = jnp.split(proj, split_at, axis=-1)

        xc = jax.nn.silu(causal_depthwise_conv(xa, w_conv[l], b_conv[l]))
        xc_h = xc.reshape(B, S, MLSTM_HEADS, MLSTM_HEAD_DIM)
        q_m = jnp.einsum('bshd,hde->bshe', xc_h, w_q_mlstm[l])
        k_m = jnp.einsum('bshd,hde->bshe', xc_h, w_k_mlstm[l]) * (MLSTM_HEAD_DIM ** -0.5)
        v_m = va.reshape(B, S, MLSTM_HEADS, MLSTM_HEAD_DIM)
        h_m = mlstm_chunkwise(q_m, k_m, v_m, ia + b_igate[l], fa + b_fgate[l])
        h_m = jax.nn.sigmoid(oa.reshape(B, S, MLSTM_HEADS, MLSTM_HEAD_DIM).astype(f32)) * h_m
        h_m = rms_norm(h_m, g_mlstm_head[l]).reshape(B, S, MLSTM_WIDTH).astype(x.dtype)
        h_m = h_m + skip_mlstm[l] * xc
        y_m = (h_m * jax.nn.silu(za)) @ w_out_mlstm[l]

        qh = partial_rotary(qa.reshape(B, S, ATTN_HEADS, ATTN_HEAD_DIM), positions)
        kh = partial_rotary(ka.reshape(B, S, ATTN_HEADS, ATTN_HEAD_DIM), positions)
        vh = vat.reshape(B, S, ATTN_HEADS, ATTN_HEAD_DIM)
        outs = []
        lses = []
        for gi, (window, dilation) in enumerate(ATTN_GROUPS):
            sl = slice(gi * HEADS_PER_GROUP, (gi + 1) * HEADS_PER_GROUP)
            o_g, lse_g = dilated_window_attention(qh[:, :, sl], kh[:, :, sl], vh[:, :, sl], window, dilation)
            outs.append(o_g)
            lses.append(lse_g)
        wts = jax.nn.softmax(jnp.stack(lses, axis=0), axis=0)
        o_a = jnp.sum(wts[..., None] * jnp.stack(outs, axis=0), axis=0)
        o_a = o_a.reshape(B, S, ATTN_OUT_WIDTH).astype(x.dtype)
        y_a = (o_a * jax.nn.silu(zb)) @ w_out_attn[l]

        merged = jax.nn.sigmoid(ga) * y_m + jax.nn.sigmoid(gb) * y_a
        x = x + gate * (merged @ w_out[l])
    return rms_norm(x, g_final)
```

```python
import functools

import jax
import jax.numpy as jnp
import numpy as np
from jax import lax
from jax.experimental import pallas as pl
from jax.experimental.pallas import tpu as pltpu

F32 = jnp.float32
BF16 = jnp.bfloat16

D_MODEL = 1024
EPS = 1e-6
M_HEADS = 4
M_HDIM = 256
M_WIDTH = M_HEADS * M_HDIM
CONV_W = 4
M_CHUNK = 256
ATTN_GROUPS = ((128, 1), (512, 4), (2048, 16))
N_GROUPS = len(ATTN_GROUPS)
A_HPG = 4
A_HDIM = 128
A_GW = A_HPG * A_HDIM
A_SPAN = 128
ROPE_DIM = A_HDIM // 4
ROPE_HALF = ROPE_DIM // 2
ROPE_THETA = 500000.0
A_TILE = 256
F_TILE = 256
LANES = 128
SUBLANES = 8
VMEM_LIMIT = 56 * 1024 * 1024

assert all(w // d == A_SPAN for w, d in ATTN_GROUPS)


def _sigmoid(x):
    return 1.0 / (1.0 + jnp.exp(-x))


def _silu(x):
    return x * _sigmoid(x)


def _log_sigmoid(x):
    return jnp.minimum(x, 0.0) - jnp.log(1.0 + jnp.exp(-jnp.abs(x)))


def _mod_norm(x, a_row, shift_row):
    ms = jnp.mean(x * x, axis=-1, keepdims=True)
    return x * lax.rsqrt(ms + EPS) * a_row + shift_row


def _dot(a, b):
    return jnp.dot(a, b, preferred_element_type=F32)


def _dot_nt(a, b):
    return lax.dot_general(a, b, (((1,), (1,)), ((), ())), preferred_element_type=F32)


def _dot_tn(a, b):
    return lax.dot_general(a, b, (((0,), (0,)), ((), ())), preferred_element_type=F32)


def _resident(shape):
    zeros = (0,) * len(shape)
    return pl.BlockSpec(shape, lambda *_: zeros, pipeline_mode=pl.Buffered(1))


def _mod_kernel(c_ref, w_ref, b_ref, g_ref, o_ref):
    j = pl.program_id(0)
    val = _dot(_silu(c_ref[...]), w_ref[...]) + b_ref[...]
    o_ref[...] = jnp.where(j == 1, g_ref[...] * (1.0 + val), val)


def _mod_call(c, w_ada, b_ada, g_norm):
    B = c.shape[0]
    return pl.pallas_call(
        _mod_kernel,
        grid=(3,),
        in_specs=[
            pl.BlockSpec((B, D_MODEL), lambda j: (0, 0)),
            pl.BlockSpec((D_MODEL, D_MODEL), lambda j: (0, j)),
            pl.BlockSpec((1, D_MODEL), lambda j: (0, j)),
            pl.BlockSpec((1, D_MODEL), lambda j: (0, 0)),
        ],
        out_specs=pl.BlockSpec((None, B, D_MODEL), lambda j: (j, 0, 0)),
        out_shape=jax.ShapeDtypeStruct((3, B, D_MODEL), F32),
        compiler_params=pltpu.CompilerParams(dimension_semantics=("arbitrary",),
                                             vmem_limit_bytes=VMEM_LIMIT),
        name="adaln_mod",
    )(c, w_ada, b_ada.reshape(1, 3 * D_MODEL), g_norm.reshape(1, D_MODEL))


def _scan_rows(x, op, identity):
    n = x.shape[0]
    row = lax.broadcasted_iota(jnp.int32, x.shape, 0)
    sh = 1
    while sh < n:
        x = op(x, jnp.where(row >= sh, pltpu.roll(x, sh, 0), identity))
        sh *= 2
    return x


def _scan_lanes(x, op, identity):
    n = x.shape[1]
    col = lax.broadcasted_iota(jnp.int32, x.shape, 1)
    sh = 1
    while sh < n:
        x = op(x, jnp.where(col >= sh, pltpu.roll(x, sh, 1), identity))
        sh *= 2
    return x


def _mlstm_kernel(x_ref, a_ref, shift_ref, wm_ref, wgc_ref, wgr_ref, bgc_ref, bgr_ref,
                  wconv_ref, bconv_ref, wq_ref, wk_ref, ghead_ref, skip_ref, wga_ref, wout_ref,
                  o_ref, c_scr, n_scr, m_scr, xa_scr, hm_scr):
    L = M_CHUNK
    W = M_WIDTH
    TAIL = SUBLANES

    @pl.when(pl.program_id(1) == 0)
    def _():
        c_scr[...] = jnp.zeros_like(c_scr)
        n_scr[...] = jnp.zeros_like(n_scr)
        m_scr[...] = jnp.zeros_like(m_scr)
        xa_scr[0:TAIL, :] = jnp.zeros((TAIL, W), F32)

    h = _mod_norm(x_ref[...], a_ref[...], shift_ref[...])
    hb = h.astype(BF16)

    xa_scr[TAIL:TAIL + L, :] = _dot(hb, wm_ref[:, 0:W])
    conv = bconv_ref[...]
    for j in range(CONV_W):
        off = TAIL - (CONV_W - 1) + j
        conv = conv + wconv_ref[j:j + 1, :] * xa_scr[off:off + L, :]
    xa_scr[0:TAIL, :] = xa_scr[L:L + TAIL, :]
    xc = _silu(conv)

    gc = _dot(hb, wgc_ref[...]) + bgc_ref[...]
    i_col = gc[:, 0:LANES]
    a_col = _scan_rows(_log_sigmoid(gc[:, LANES:2 * LANES]), jnp.add, 0.0)
    r_col = i_col - a_col
    m_prev = m_scr[0:1, :]
    big_m = jnp.maximum(_scan_rows(r_col, jnp.maximum, -jnp.inf), m_prev)
    m_last = big_m[L - 1:L, :]
    g_tot = a_col[L - 1:L, :]
    gr = _dot_nt(wgr_ref[...], hb) + bgr_ref[...]
    a_row = _scan_lanes(_log_sigmoid(gr[SUBLANES:2 * SUBLANES, :]), jnp.add, 0.0)
    r_row = gr[0:SUBLANES, :] - a_row

    w_inter_all = jnp.exp(m_prev - big_m)
    floor_all = jnp.exp(-a_col - big_m)
    kscale_all = jnp.exp(r_col - m_last)
    decay_all = jnp.exp(m_prev - m_last)
    m_scr[0:1, :] = g_tot + m_last

    causal = (lax.broadcasted_iota(jnp.int32, (L, L), 0) >=
              lax.broadcasted_iota(jnp.int32, (L, L), 1))

    for hd in range(M_HEADS):
        cs = slice(hd * M_HDIM, (hd + 1) * M_HDIM)
        xc_h = xc[:, cs]
        xcb = xc_h.astype(BF16)
        qf = _dot(xcb, wq_ref[hd])
        qb = qf.astype(BF16)
        kf = _dot(xcb, wk_ref[hd])
        kb = kf.astype(BF16)
        vb = _dot(hb, wm_ref[:, W + hd * M_HDIM:W + (hd + 1) * M_HDIM]).astype(BF16)

        w_intra = jnp.exp(jnp.where(causal, r_row[hd:hd + 1, :] - big_m[:, hd:hd + 1], -jnp.inf))
        s = _dot_nt(qb, kb) * w_intra
        w_inter = w_inter_all[:, hd:hd + 1]
        c_old = c_scr[hd]
        n_old = n_scr[hd:hd + 1, :]
        num = _dot(s.astype(BF16), vb) + w_inter * _dot(qb, c_old.astype(BF16))
        den = (jnp.sum(s, axis=1, keepdims=True) +
               w_inter * jnp.sum(qf * n_old, axis=1, keepdims=True))
        hout = num / jnp.maximum(jnp.abs(den), floor_all[:, hd:hd + 1])

        kw = kscale_all[:, hd:hd + 1] * kf
        decay = decay_all[:, hd:hd + 1]
        c_scr[hd] = decay * c_old + _dot_tn(kw.astype(BF16), vb)
        n_scr[hd:hd + 1, :] = decay * n_old + jnp.sum(kw, axis=0, keepdims=True)

        oa = _dot(hb, wm_ref[:, 2 * W + hd * M_HDIM:2 * W + (hd + 1) * M_HDIM])
        hm = _sigmoid(oa) * hout
        hm = hm * lax.rsqrt(jnp.mean(hm * hm, axis=-1, keepdims=True) + EPS) * ghead_ref[:, cs]
        hm = hm + skip_ref[:, cs] * xc_h
        za = _dot(hb, wm_ref[:, 3 * W + hd * M_HDIM:3 * W + (hd + 1) * M_HDIM])
        hm_scr[:, cs] = (hm * _silu(za)).astype(BF16)

    y_m = _dot(hm_scr[...], wout_ref[...])
    ga = _dot(hb, wga_ref[...])
    o_ref[...] = (_sigmoid(ga) * y_m).astype(o_ref.dtype)


def _mlstm_call(x, a_mod, shift_mod, wm, wgc, wgr, bgc, bgr, wconv, bconv, wq, wk, ghead, skip,
                wga, wout):
    B, S, D = x.shape
    L = M_CHUNK
    assert S % L == 0
    tok = pl.BlockSpec((None, L, D), lambda b, c: (b, c, 0))
    per_batch = pl.BlockSpec((None, 1, D), lambda b, c: (b, 0, 0))
    args = (wm, wgc, wgr, bgc, bgr, wconv, bconv, wq, wk, ghead, skip, wga, wout)
    return pl.pallas_call(
        _mlstm_kernel,
        grid=(B, S // L),
        in_specs=[tok, per_batch, per_batch] + [_resident(a.shape) for a in args],
        out_specs=pl.BlockSpec((None, L, D), lambda b, c: (b, c, 0)),
        out_shape=jax.ShapeDtypeStruct((B, S, D), BF16),
        scratch_shapes=[
            pltpu.VMEM((M_HEADS, M_HDIM, M_HDIM), F32),
            pltpu.VMEM((SUBLANES, M_HDIM), F32),
            pltpu.VMEM((SUBLANES, LANES), F32),
            pltpu.VMEM((L + SUBLANES, M_WIDTH), F32),
            pltpu.VMEM((L, M_WIDTH), BF16),
        ],
        compiler_params=pltpu.CompilerParams(dimension_semantics=("arbitrary", "arbitrary"),
                                             vmem_limit_bytes=VMEM_LIMIT),
        name="mlstm_branch",
    )(x, a_mod, shift_mod, *args)


def _attn_kernel(x_ref, a_ref, shift_ref, pos_ref, freq_ref, esel_ref, w_ref,
                 o_ref, lse_ref, k_scr, v_scr):
    n = A_TILE
    ub = pl.program_id(2)

    @pl.when(ub == 0)
    def _():
        k_scr[0:A_SPAN, :] = jnp.zeros((A_SPAN, A_GW), BF16)
        v_scr[0:A_SPAN, :] = jnp.zeros((A_SPAN, A_GW), BF16)

    @pl.when(ub > 0)
    def _():
        k_scr[0:A_SPAN, :] = k_scr[n:n + A_SPAN, :]
        v_scr[0:A_SPAN, :] = v_scr[n:n + A_SPAN, :]

    h = _mod_norm(x_ref[...], a_ref[...], shift_ref[...])
    hb = h.astype(BF16)

    ang = freq_ref[...] * pos_ref[...].astype(F32)
    cs = jnp.concatenate([jnp.cos(ang), jnp.sin(ang)], axis=0)
    cs_hi = cs.astype(BF16)
    cs_lo = (cs - cs_hi.astype(F32)).astype(BF16)
    tab = _dot_tn(jnp.concatenate([cs_hi, cs_lo], axis=0), esel_ref[...])
    lane = lax.broadcasted_iota(jnp.int32, (1, LANES), 1)
    rot_lane = (lane % (LANES // 2)) < ROPE_HALF
    cos_t = tab[:, 0:LANES] + jnp.where(rot_lane, 0.0, 1.0)
    sin_t = tab[:, LANES:2 * LANES]

    def rope(t):
        return t * cos_t + pltpu.roll(t, LANES // 2, 1) * sin_t

    scale = A_HDIM ** -0.5
    q = _dot(hb, w_ref[:, 0:A_GW])
    k = _dot(hb, w_ref[:, A_GW:2 * A_GW])
    qs = []
    for hd in range(A_HPG):
        cs_ = slice(hd * A_HDIM, (hd + 1) * A_HDIM)
        qs.append((rope(q[:, cs_]) * scale).astype(BF16))
        k_scr[A_SPAN:A_SPAN + n, cs_] = rope(k[:, cs_]).astype(BF16)
    v_scr[A_SPAN:A_SPAN + n, :] = _dot(hb, w_ref[:, 2 * A_GW:3 * A_GW]).astype(BF16)

    p_idx = lax.broadcasted_iota(jnp.int32, (A_SPAN, 2 * A_SPAN), 0)
    c_idx = lax.broadcasted_iota(jnp.int32, (A_SPAN, 2 * A_SPAN), 1)
    band = (c_idx >= p_idx) & (c_idx <= p_idx + A_SPAN)
    lane_o = lax.broadcasted_iota(jnp.int32, (A_SPAN, LANES), 1)
    for jb in range(n // A_SPAN):
        rows = slice(jb * A_SPAN, (jb + 1) * A_SPAN)
        krows = slice(jb * A_SPAN, (jb + 2) * A_SPAN)
        if jb == 0:
            valid = band & (c_idx >= jnp.where(ub == 0, A_SPAN, 0))
        else:
            valid = band
        lse_blk = jnp.zeros((A_SPAN, LANES), F32)
        for hd in range(A_HPG):
            cs_ = slice(hd * A_HDIM, (hd + 1) * A_HDIM)
            s = _dot_nt(qs[hd][rows, :], k_scr[krows, cs_])
            s = jnp.where(valid, s, -jnp.inf)
            mx = jnp.max(s, axis=1, keepdims=True)
            e = jnp.exp(s - mx)
            den = jnp.sum(e, axis=1, keepdims=True)
            o = _dot(e.astype(BF16), v_scr[krows, cs_]) / den
            o_ref[rows, cs_] = o.astype(o_ref.dtype)
            lse_blk = jnp.where(lane_o == hd, mx + jnp.log(den), lse_blk)
        lse_ref[rows, :] = lse_blk


def _attn_call(x, a_mod, shift_mod, positions, freq, esel, w_qkv, dilation):
    B, S, D = x.shape
    d = dilation
    U = S // d
    n = A_TILE
    assert S % d == 0 and U % n == 0
    xv = x.reshape(B, U, d * D)
    pos = positions.reshape(B, U, d).transpose(0, 2, 1).reshape(B, d, 1, U)
    per_batch = pl.BlockSpec((None, 1, D), lambda b, r, u: (b, 0, 0))
    o, lse = pl.pallas_call(
        _attn_kernel,
        grid=(B, d, U // n),
        in_specs=[
            pl.BlockSpec((None, n, D), lambda b, r, u: (b, u, r)),
            per_batch, per_batch,
            pl.BlockSpec((None, None, 1, n), lambda b, r, u: (b, r, 0, u)),
            _resident(freq.shape), _resident(esel.shape), _resident(w_qkv.shape),
        ],
        out_specs=[
            pl.BlockSpec((None, n, A_GW), lambda b, r, u: (b, u, r)),
            pl.BlockSpec((None, n, LANES), lambda b, r, u: (b, u, r)),
        ],
        out_shape=[
            jax.ShapeDtypeStruct((B, U, d * A_GW), F32),
            jax.ShapeDtypeStruct((B, U, d * LANES), F32),
        ],
        scratch_shapes=[
            pltpu.VMEM((A_SPAN + n, A_GW), BF16),
            pltpu.VMEM((A_SPAN + n, A_GW), BF16),
        ],
        compiler_params=pltpu.CompilerParams(
            dimension_semantics=("arbitrary", "arbitrary", "arbitrary"),
            vmem_limit_bytes=VMEM_LIMIT),
        name=f"dilated_attn_d{d}",
    )(xv, a_mod, shift_mod, pos, freq, esel, w_qkv)
    return o.reshape(B, S, A_GW), lse.reshape(B, S, LANES)


def _final_kernel(x_ref, a_ref, shift_ref, gate_ref, ym_ref, o1_ref, o2_ref, o3_ref,
                  l1_ref, l2_ref, l3_ref, wz_ref, wgb_ref, woa_ref, wout_ref, gfin_ref,
                  out_ref, oz_scr):
    x = x_ref[...]
    hb = _mod_norm(x, a_ref[...], shift_ref[...]).astype(BF16)

    l1, l2, l3 = l1_ref[...], l2_ref[...], l3_ref[...]
    mx = jnp.maximum(jnp.maximum(l1, l2), l3)
    e1, e2, e3 = jnp.exp(l1 - mx), jnp.exp(l2 - mx), jnp.exp(l3 - mx)
    inv = 1.0 / (e1 + e2 + e3)
    w1, w2, w3 = e1 * inv, e2 * inv, e3 * inv

    zb = _dot(hb, wz_ref[...])
    for hd in range(A_HPG):
        cs = slice(hd * A_HDIM, (hd + 1) * A_HDIM)
        o_a = (w1[:, hd:hd + 1] * o1_ref[:, cs] + w2[:, hd:hd + 1] * o2_ref[:, cs] +
               w3[:, hd:hd + 1] * o3_ref[:, cs])
        oz_scr[:, cs] = (o_a * _silu(zb[:, cs])).astype(BF16)
    y_a = _dot(oz_scr[...], woa_ref[...])
    gb = _dot(hb, wgb_ref[...])
    merged = ym_ref[...].astype(F32) + _sigmoid(gb) * y_a
    xn = x + gate_ref[...] * _dot(merged.astype(BF16), wout_ref[...])
    out_ref[...] = xn * lax.rsqrt(jnp.mean(xn * xn, axis=-1, keepdims=True) + EPS) * gfin_ref[...]


def _final_call(x, a_mod, shift_mod, gate_mod, ym, os_, ls_, wz, wgb, woa, wout, gfin):
    B, S, D = x.shape
    T = F_TILE
    assert S % T == 0
    tok = lambda w: pl.BlockSpec((None, T, w), lambda b, t: (b, t, 0))
    per_batch = pl.BlockSpec((None, 1, D), lambda b, t: (b, 0, 0))
    wts = (wz, wgb, woa, wout, gfin)
    return pl.pallas_call(
        _final_kernel,
        grid=(B, S // T),
        in_specs=([tok(D), per_batch, per_batch, per_batch, tok(D)] + [tok(A_GW)] * N_GROUPS +
                  [tok(LANES)] * N_GROUPS + [_resident(a.shape) for a in wts]),
        out_specs=tok(D),
        out_shape=jax.ShapeDtypeStruct((B, S, D), F32),
        scratch_shapes=[pltpu.VMEM((T, A_GW), BF16)],
        compiler_params=pltpu.CompilerParams(dimension_semantics=("arbitrary", "arbitrary"),
                                             vmem_limit_bytes=VMEM_LIMIT),
        name="merge_out_norm",
    )(x, a_mod, shift_mod, gate_mod, ym, *os_, *ls_, *wts)


def _rope_head_perm():
    rest = np.arange(ROPE_DIM, A_HDIM)
    half_rest = (A_HDIM // 2) - ROPE_HALF
    return np.concatenate([np.arange(0, ROPE_HALF), rest[:half_rest],
                           np.arange(ROPE_HALF, ROPE_DIM), rest[half_rest:]])


def _rope_tables():
    inv_freq = ROPE_THETA ** (-jnp.arange(ROPE_HALF, dtype=F32) / ROPE_HALF)
    e = np.zeros((4 * ROPE_HALF, 2 * LANES), np.float32)
    for part in range(2):
        for i in range(ROPE_HALF):
            rc, rs = part * 2 * ROPE_HALF + i, part * 2 * ROPE_HALF + ROPE_HALF + i
            e[rc, i] = 1.0
            e[rc, LANES // 2 + i] = 1.0
            e[rs, LANES + i] = -1.0
            e[rs, LANES + LANES // 2 + i] = 1.0
    return inv_freq.reshape(ROPE_HALF, 1), jnp.asarray(e, BF16)


def kernel(x, c, positions, w_ada, b_ada, g_norm, w_in, b_igate, b_fgate, w_conv, b_conv,
           w_q_mlstm, w_k_mlstm, g_mlstm_head, skip_mlstm, w_out_mlstm, w_out_attn, w_out, g_final):
    B, S, D = x.shape
    assert D == D_MODEL and w_ada.shape[0] == 1
    W = M_WIDTH
    QKV = N_GROUPS * A_GW
    o_gate = 4 * W
    o_q = o_gate + 2 * M_HEADS
    o_k, o_v = o_q + QKV, o_q + 2 * QKV
    o_zb = o_q + 3 * QKV
    o_ga = o_zb + A_GW
    o_gb = o_ga + D
    w = w_in[0]

    mod = _mod_call(c, w_ada[0], b_ada[0], g_norm[0])
    shift_mod = mod[0].reshape(B, 1, D)
    a_mod = mod[1].reshape(B, 1, D)
    gate_mod = mod[2].reshape(B, 1, D)

    w_i = w[:, o_gate:o_gate + M_HEADS]
    w_f = w[:, o_gate + M_HEADS:o_gate + 2 * M_HEADS]
    pad_c = jnp.zeros((D, LANES - M_HEADS), F32)
    wgc = jnp.concatenate([w_i, pad_c, w_f, pad_c], axis=1).astype(BF16)
    pad_r = jnp.zeros((SUBLANES - M_HEADS, D), F32)
    wgr = jnp.concatenate([w_i.T, pad_r, w_f.T, pad_r], axis=0).astype(BF16)
    zc = jnp.zeros((LANES - M_HEADS,), F32)
    bgc = jnp.concatenate([b_igate[0], zc, b_fgate[0], zc]).reshape(1, 2 * LANES)
    zr = jnp.zeros((SUBLANES - M_HEADS,), F32)
    bgr = jnp.concatenate([b_igate[0], zr, b_fgate[0], zr]).reshape(2 * SUBLANES, 1)
    ym = _mlstm_call(
        x, a_mod, shift_mod, w[:, 0:4 * W].astype(BF16), wgc, wgr, bgc, bgr,
        w_conv[0], b_conv[0].reshape(1, W), w_q_mlstm[0].astype(BF16),
        (w_k_mlstm[0] * (M_HDIM ** -0.5)).astype(BF16),
        g_mlstm_head[0].reshape(1, W), skip_mlstm[0].reshape(1, W),
        w[:, o_ga:o_ga + D].astype(BF16), w_out_mlstm[0].astype(BF16))

    freq, esel = _rope_tables()
    perm = _rope_head_perm()
    os_, ls_ = [], []
    for gi, (_, dilation) in enumerate(ATTN_GROUPS):
        cols = (gi * A_GW + np.arange(A_GW).reshape(A_HPG, A_HDIM))
        qk_cols = cols[:, perm].reshape(-1)
        w_qkv = jnp.concatenate([w[:, o_q + qk_cols], w[:, o_k + qk_cols],
                                 w[:, o_v + gi * A_GW:o_v + (gi + 1) * A_GW]], axis=1).astype(BF16)
        o_g, l_g = _attn_call(x, a_mod, shift_mod, positions, freq, esel, w_qkv, dilation)
        os_.append(o_g)
        ls_.append(l_g)

    return _final_call(
        x, a_mod, shift_mod, gate_mod, ym, os_, ls_,
        w[:, o_zb:o_zb + A_GW].astype(BF16), w[:, o_gb:o_gb + D].astype(BF16),
        w_out_attn[0].astype(BF16), w_out[0].astype(BF16), g_final.reshape(1, D))
```

```python
import functools

import jax
import jax.numpy as jnp
import numpy as np
from jax import lax
from jax.experimental import pallas as pl
from jax.experimental.pallas import tpu as pltpu

F32 = jnp.float32
BF16 = jnp.bfloat16

D_MODEL = 1024
EPS = 1e-6
M_HEADS = 4
M_HDIM = 256
M_WIDTH = M_HEADS * M_HDIM
CONV_W = 4
M_CHUNK = 256
ATTN_GROUPS = ((128, 1), (512, 4), (2048, 16))
N_GROUPS = len(ATTN_GROUPS)
A_HPG = 4
A_HDIM = 128
A_GW = A_HPG * A_HDIM
A_SPAN = 128
ROPE_DIM = A_HDIM // 4
ROPE_HALF = ROPE_DIM // 2
ROPE_THETA = 500000.0
A_TILE = 256
F_TILE = 256
LANES = 128
SUBLANES = 8
VMEM_LIMIT = 56 * 1024 * 1024

assert all(w // d == A_SPAN for w, d in ATTN_GROUPS)


def _sigmoid(x):
    return 1.0 / (1.0 + jnp.exp(-x))


def _silu(x):
    return x * _sigmoid(x)


def _log_sigmoid(x):
    return jnp.minimum(x, 0.0) - jnp.log(1.0 + jnp.exp(-jnp.abs(x)))


def _mod_norm(x, a_row, shift_row):
    ms = jnp.mean(x * x, axis=-1, keepdims=True)
    return x * lax.rsqrt(ms + EPS) * a_row + shift_row


def _dot(a, b):
    return jnp.dot(a, b, preferred_element_type=F32)


def _dot_nt(a, b):
    return lax.dot_general(a, b, (((1,), (1,)), ((), ())), preferred_element_type=F32)


def _dot_tn(a, b):
    return lax.dot_general(a, b, (((0,), (0,)), ((), ())), preferred_element_type=F32)


def _resident(shape):
    zeros = (0,) * len(shape)
    return pl.BlockSpec(shape, lambda *_: zeros, pipeline_mode=pl.Buffered(1))


def _mod_kernel(c_ref, w_ref, b_ref, g_ref, o_ref):
    j = pl.program_id(0)
    val = _dot(_silu(c_ref[...]), w_ref[...]) + b_ref[...]
    o_ref[...] = jnp.where(j == 1, g_ref[...] * (1.0 + val), val)


def _mod_call(c, w_ada, b_ada, g_norm):
    B = c.shape[0]
    return pl.pallas_call(
        _mod_kernel,
        grid=(3,),
        in_specs=[
            pl.BlockSpec((B, D_MODEL), lambda j: (0, 0)),
            pl.BlockSpec((D_MODEL, D_MODEL), lambda j: (0, j)),
            pl.BlockSpec((1, D_MODEL), lambda j: (0, j)),
            pl.BlockSpec((1, D_MODEL), lambda j: (0, 0)),
        ],
        out_specs=pl.BlockSpec((None, B, D_MODEL), lambda j: (j, 0, 0)),
        out_shape=jax.ShapeDtypeStruct((3, B, D_MODEL), F32),
        compiler_params=pltpu.CompilerParams(dimension_semantics=("arbitrary",),
                                             vmem_limit_bytes=VMEM_LIMIT),
        name="adaln_mod",
    )(c, w_ada, b_ada.reshape(1, 3 * D_MODEL), g_norm.reshape(1, D_MODEL))


def _scan_rows(x, op, identity):
    n = x.shape[0]
    row = lax.broadcasted_iota(jnp.int32, x.shape, 0)
    sh = 1
    while sh < n:
        x = op(x, jnp.where(row >= sh, pltpu.roll(x, sh, 0), identity))
        sh *= 2
    return x


def _scan_lanes(x, op, identity):
    n = x.shape[1]
    col = lax.broadcasted_iota(jnp.int32, x.shape, 1)
    sh = 1
    while sh < n:
        x = op(x, jnp.where(col >= sh, pltpu.roll(x, sh, 1), identity))
        sh *= 2
    return x


def _mlstm_kernel(x_ref, a_ref, shift_ref, wm_ref, wgc_ref, wgr_ref, bgc_ref, bgr_ref,
                  wconv_ref, bconv_ref, wq_ref, wk_ref, ghead_ref, skip_ref, wga_ref, wout_ref,
                  o_ref, hnat_ref, hd4_ref, hd16_ref, c_scr, n_scr, m_scr, xa_scr, hm_scr, h_scr):
    L = M_CHUNK
    W = M_WIDTH
    TAIL = SUBLANES

    @pl.when(pl.program_id(1) == 0)
    def _():
        c_scr[...] = jnp.zeros_like(c_scr)
        n_scr[...] = jnp.zeros_like(n_scr)
        m_scr[...] = jnp.zeros_like(m_scr)
        xa_scr[0:TAIL, :] = jnp.zeros((TAIL, W), F32)

    h = _mod_norm(x_ref[...], a_ref[...], shift_ref[...])
    hb = h.astype(BF16)

    hnat_ref[...] = hb
    for j in range(D_MODEL // LANES):
        ls = slice(j * LANES, (j + 1) * LANES)
        h_scr[j] = h[:, ls]
        for d, ref in ((ATTN_GROUPS[1][1], hd4_ref), (ATTN_GROUPS[2][1], hd16_ref)):
            for r in range(d):
                ref[r, :, ls] = h_scr[j, pl.ds(r, L // d, stride=d), :].astype(BF16)

    xa_scr[TAIL:TAIL + L, :] = _dot(hb, wm_ref[:, 0:W])
    conv = bconv_ref[...]
    for j in range(CONV_W):
        off = TAIL - (CONV_W - 1) + j
        conv = conv + wconv_ref[j:j + 1, :] * xa_scr[off:off + L, :]
    xa_scr[0:TAIL, :] = xa_scr[L:L + TAIL, :]
    xc = _silu(conv)

    gc = _dot(hb, wgc_ref[...]) + bgc_ref[...]
    i_col = gc[:, 0:LANES]
    a_col = _scan_rows(_log_sigmoid(gc[:, LANES:2 * LANES]), jnp.add, 0.0)
    r_col = i_col - a_col
    m_prev = m_scr[0:1, :]
    big_m = jnp.maximum(_scan_rows(r_col, jnp.maximum, -jnp.inf), m_prev)
    m_last = big_m[L - 1:L, :]
    g_tot = a_col[L - 1:L, :]
    gr = _dot_nt(wgr_ref[...], hb) + bgr_ref[...]
    a_row = _scan_lanes(_log_sigmoid(gr[SUBLANES:2 * SUBLANES, :]), jnp.add, 0.0)
    r_row = gr[0:SUBLANES, :] - a_row

    w_inter_all = jnp.exp(m_prev - big_m)
    floor_all = jnp.exp(-a_col - big_m)
    kscale_all = jnp.exp(r_col - m_last)
    decay_all = jnp.exp(m_prev - m_last)
    m_scr[0:1, :] = g_tot + m_last

    causal = (lax.broadcasted_iota(jnp.int32, (L, L), 0) >=
              lax.broadcasted_iota(jnp.int32, (L, L), 1))

    for hd in range(M_HEADS):
        cs = slice(hd * M_HDIM, (hd + 1) * M_HDIM)
        xc_h = xc[:, cs]
        xcb = xc_h.astype(BF16)
        qf = _dot(xcb, wq_ref[hd])
        qb = qf.astype(BF16)
        kf = _dot(xcb, wk_ref[hd])
        kb = kf.astype(BF16)
        vb = _dot(hb, wm_ref[:, W + hd * M_HDIM:W + (hd + 1) * M_HDIM]).astype(BF16)

        w_intra = jnp.exp(jnp.where(causal, r_row[hd:hd + 1, :] - big_m[:, hd:hd + 1], -jnp.inf))
        s = _dot_nt(qb, kb) * w_intra
        w_inter = w_inter_all[:, hd:hd + 1]
        c_old = c_scr[hd]
        n_old = n_scr[hd:hd + 1, :]
        num = _dot(s.astype(BF16), vb) + w_inter * _dot(qb, c_old.astype(BF16))
        den = (jnp.sum(s, axis=1, keepdims=True) +
               w_inter * jnp.sum(qf * n_old, axis=1, keepdims=True))
        hout = num / jnp.maximum(jnp.abs(den), floor_all[:, hd:hd + 1])

        kw = kscale_all[:, hd:hd + 1] * kf
        decay = decay_all[:, hd:hd + 1]
        c_scr[hd] = decay * c_old + _dot_tn(kw.astype(BF16), vb)
        n_scr[hd:hd + 1, :] = decay * n_old + jnp.sum(kw, axis=0, keepdims=True)

        oa = _dot(hb, wm_ref[:, 2 * W + hd * M_HDIM:2 * W + (hd + 1) * M_HDIM])
        hm = _sigmoid(oa) * hout
        hm = hm * lax.rsqrt(jnp.mean(hm * hm, axis=-1, keepdims=True) + EPS) * ghead_ref[:, cs]
        hm = hm + skip_ref[:, cs] * xc_h
        za = _dot(hb, wm_ref[:, 3 * W + hd * M_HDIM:3 * W + (hd + 1) * M_HDIM])
        hm_scr[:, cs] = (hm * _silu(za)).astype(BF16)

    y_m = _dot(hm_scr[...], wout_ref[...])
    ga = _dot(hb, wga_ref[...])
    o_ref[...] = (_sigmoid(ga) * y_m).astype(o_ref.dtype)


def _mlstm_call(x, a_mod, shift_mod, wm, wgc, wgr, bgc, bgr, wconv, bconv, wq, wk, ghead, skip,
                wga, wout):
    B, S, D = x.shape
    L = M_CHUNK
    d4, d16 = ATTN_GROUPS[1][1], ATTN_GROUPS[2][1]
    assert S % L == 0 and L % (d16 * 2 * SUBLANES) == 0
    tok = pl.BlockSpec((None, L, D), lambda b, c: (b, c, 0))
    per_batch = pl.BlockSpec((None, 1, D), lambda b, c: (b, 0, 0))
    split = lambda d: pl.BlockSpec((None, d, L // d, D), lambda b, c: (b, 0, c, 0))
    args = (wm, wgc, wgr, bgc, bgr, wconv, bconv, wq, wk, ghead, skip, wga, wout)
    return pl.pallas_call(
        _mlstm_kernel,
        grid=(B, S // L),
        in_specs=[tok, per_batch, per_batch] + [_resident(a.shape) for a in args],
        out_specs=[tok, tok, split(d4), split(d16)],
        out_shape=[
            jax.ShapeDtypeStruct((B, S, D), BF16),
            jax.ShapeDtypeStruct((B, S, D), BF16),
            jax.ShapeDtypeStruct((B, d4, S // d4, D), BF16),
            jax.ShapeDtypeStruct((B, d16, S // d16, D), BF16),
        ],
        scratch_shapes=[
            pltpu.VMEM((M_HEADS, M_HDIM, M_HDIM), F32),
            pltpu.VMEM((SUBLANES, M_HDIM), F32),
            pltpu.VMEM((SUBLANES, LANES), F32),
            pltpu.VMEM((L + SUBLANES, M_WIDTH), F32),
            pltpu.VMEM((L, M_WIDTH), BF16),
            pltpu.VMEM((D // LANES, L, LANES), F32),
        ],
        compiler_params=pltpu.CompilerParams(dimension_semantics=("arbitrary", "arbitrary"),
                                             vmem_limit_bytes=VMEM_LIMIT),
        name="mlstm_branch",
    )(x, a_mod, shift_mod, *args)


def _attn_kernel(h_ref, pos_ref, freq_ref, esel_ref, w_ref, o_ref, lse_ref, k_scr, v_scr):
    n = A_TILE
    ub = pl.program_id(2)

    @pl.when(ub == 0)
    def _():
        k_scr[0:A_SPAN, :] = jnp.zeros((A_SPAN, A_GW), BF16)
        v_scr[0:A_SPAN, :] = jnp.zeros((A_SPAN, A_GW), BF16)

    @pl.when(ub > 0)
    def _():
        k_scr[0:A_SPAN, :] = k_scr[n:n + A_SPAN, :]
        v_scr[0:A_SPAN, :] = v_scr[n:n + A_SPAN, :]

    hb = h_ref[...]

    ang = freq_ref[...] * pos_ref[...].astype(F32)
    cs = jnp.concatenate([jnp.cos(ang), jnp.sin(ang)], axis=0)
    cs_hi = cs.astype(BF16)
    cs_lo = (cs - cs_hi.astype(F32)).astype(BF16)
    tab = _dot_tn(jnp.concatenate([cs_hi, cs_lo], axis=0), esel_ref[...])
    lane = lax.broadcasted_iota(jnp.int32, (1, LANES), 1)
    rot_lane = (lane % (LANES // 2)) < ROPE_HALF
    cos_t = tab[:, 0:LANES] + jnp.where(rot_lane, 0.0, 1.0)
    sin_t = tab[:, LANES:2 * LANES]

    def rope(t):
        return t * cos_t + pltpu.roll(t, LANES // 2, 1) * sin_t

    scale = A_HDIM ** -0.5
    q = _dot(hb, w_ref[:, 0:A_GW])
    k = _dot(hb, w_ref[:, A_GW:2 * A_GW])
    qs = []
    for hd in range(A_HPG):
        cs_ = slice(hd * A_HDIM, (hd + 1) * A_HDIM)
        qs.append((rope(q[:, cs_]) * scale).astype(BF16))
        k_scr[A_SPAN:A_SPAN + n, cs_] = rope(k[:, cs_]).astype(BF16)
    v_scr[A_SPAN:A_SPAN + n, :] = _dot(hb, w_ref[:, 2 * A_GW:3 * A_GW]).astype(BF16)

    p_idx = lax.broadcasted_iota(jnp.int32, (A_SPAN, 2 * A_SPAN), 0)
    c_idx = lax.broadcasted_iota(jnp.int32, (A_SPAN, 2 * A_SPAN), 1)
    band = (c_idx >= p_idx) & (c_idx <= p_idx + A_SPAN)
    lane_o = lax.broadcasted_iota(jnp.int32, (A_SPAN, LANES), 1)
    for jb in range(n // A_SPAN):
        rows = slice(jb * A_SPAN, (jb + 1) * A_SPAN)
        krows = slice(jb * A_SPAN, (jb + 2) * A_SPAN)
        if jb == 0:
            valid = band & (c_idx >= jnp.where(ub == 0, A_SPAN, 0))
        else:
            valid = band
        lse_blk = jnp.zeros((A_SPAN, LANES), F32)
        for hd in range(A_HPG):
            cs_ = slice(hd * A_HDIM, (hd + 1) * A_HDIM)
            s = _dot_nt(qs[hd][rows, :], k_scr[krows, cs_])
            s = jnp.where(valid, s, -jnp.inf)
            mx = jnp.max(s, axis=1, keepdims=True)
            e = jnp.exp(s - mx)
            den = jnp.sum(e, axis=1, keepdims=True)
            o = _dot(e.astype(BF16), v_scr[krows, cs_]) / den
            o_ref[rows, cs_] = o.astype(o_ref.dtype)
            lse_blk = jnp.where(lane_o == hd, mx + jnp.log(den), lse_blk)
        lse_ref[rows, :] = lse_blk


def _attn_call(h_split, positions, freq, esel, w_qkv):
    B, d, U, D = h_split.shape
    n = A_TILE
    assert U % n == 0
    pos = positions.reshape(B, U, d).transpose(0, 2, 1).reshape(B, d, 1, U)
    tok = lambda w: pl.BlockSpec((None, None, n, w), lambda b, r, u: (b, r, u, 0))
    return pl.pallas_call(
        _attn_kernel,
        grid=(B, d, U // n),
        in_specs=[
            tok(D),
            pl.BlockSpec((None, None, 1, n), lambda b, r, u: (b, r, 0, u)),
            _resident(freq.shape), _resident(esel.shape), _resident(w_qkv.shape),
        ],
        out_specs=[tok(A_GW), tok(LANES)],
        out_shape=[
            jax.ShapeDtypeStruct((B, d, U, A_GW), F32),
            jax.ShapeDtypeStruct((B, d, U, LANES), F32),
        ],
        scratch_shapes=[
            pltpu.VMEM((A_SPAN + n, A_GW), BF16),
            pltpu.VMEM((A_SPAN + n, A_GW), BF16),
        ],
        compiler_params=pltpu.CompilerParams(
            dimension_semantics=("arbitrary", "arbitrary", "arbitrary"),
            vmem_limit_bytes=VMEM_LIMIT),
        name=f"dilated_attn_d{d}",
    )(h_split, pos, freq, esel, w_qkv)


def _final_kernel(x_ref, h_ref, gate_ref, ym_ref, o1_ref, o2_ref, o3_ref, l1_ref, l2_ref, l3_ref,
                  wz_ref, wgb_ref, woa_ref, wout_ref, gfin_ref, out_ref, oz_scr, o_scr, l_scr):
    T = F_TILE
    x = x_ref[...]
    hb = h_ref[...]

    for gi, (o_ref, l_ref) in enumerate(((o1_ref, l1_ref), (o2_ref, l2_ref), (o3_ref, l3_ref))):
        d = ATTN_GROUPS[gi][1]
        for r in range(d):
            rows = pl.ds(r, T // d, stride=d)
            l_scr[gi, rows, :] = l_ref[r]
            for hd in range(A_HPG):
                o_scr[gi * A_HPG + hd, rows, :] = o_ref[r, :, hd * A_HDIM:(hd + 1) * A_HDIM]

    l1, l2, l3 = l_scr[0], l_scr[1], l_scr[2]
    mx = jnp.maximum(jnp.maximum(l1, l2), l3)
    e1, e2, e3 = jnp.exp(l1 - mx), jnp.exp(l2 - mx), jnp.exp(l3 - mx)
    inv = 1.0 / (e1 + e2 + e3)
    w1, w2, w3 = e1 * inv, e2 * inv, e3 * inv

    zb = _dot(hb, wz_ref[...])
    for hd in range(A_HPG):
        cs = slice(hd * A_HDIM, (hd + 1) * A_HDIM)
        o_a = (w1[:, hd:hd + 1] * o_scr[hd] + w2[:, hd:hd + 1] * o_scr[A_HPG + hd] +
               w3[:, hd:hd + 1] * o_scr[2 * A_HPG + hd])
        oz_scr[:, cs] = (o_a * _silu(zb[:, cs])).astype(BF16)
    y_a = _dot(oz_scr[...], woa_ref[...])
    gb = _dot(hb, wgb_ref[...])
    merged = ym_ref[...].astype(F32) + _sigmoid(gb) * y_a
    xn = x + gate_ref[...] * _dot(merged.astype(BF16), wout_ref[...])
    out_ref[...] = xn * lax.rsqrt(jnp.mean(xn * xn, axis=-1, keepdims=True) + EPS) * gfin_ref[...]


def _final_call(x, h_nat, gate_mod, ym, os_, ls_, wz, wgb, woa, wout, gfin):
    B, S, D = x.shape
    T = F_TILE
    assert S % T == 0 and all(T % (d * SUBLANES) == 0 for _, d in ATTN_GROUPS)
    tok = lambda w: pl.BlockSpec((None, T, w), lambda b, t: (b, t, 0))
    split = lambda a: pl.BlockSpec((None, a.shape[1], T // a.shape[1], a.shape[3]),
                                   lambda b, t: (b, 0, t, 0))
    per_batch = pl.BlockSpec((None, 1, D), lambda b, t: (b, 0, 0))
    wts = (wz, wgb, woa, wout, gfin)
    return pl.pallas_call(
        _final_kernel,
        grid=(B, S // T),
        in_specs=([tok(D), tok(D), per_batch, tok(D)] + [split(a) for a in os_] +
                  [split(a) for a in ls_] + [_resident(a.shape) for a in wts]),
        out_specs=tok(D),
        out_shape=jax.ShapeDtypeStruct((B, S, D), F32),
        scratch_shapes=[
            pltpu.VMEM((T, A_GW), BF16),
            pltpu.VMEM((N_GROUPS * A_HPG, T, A_HDIM), F32),
            pltpu.VMEM((N_GROUPS, T, LANES), F32),
        ],
        compiler_params=pltpu.CompilerParams(dimension_semantics=("arbitrary", "arbitrary"),
                                             vmem_limit_bytes=VMEM_LIMIT),
        name="merge_out_norm",
    )(x, h_nat, gate_mod, ym, *os_, *ls_, *wts)


def _rope_head_perm():
    rest = np.arange(ROPE_DIM, A_HDIM)
    half_rest = (A_HDIM // 2) - ROPE_HALF
    return np.concatenate([np.arange(0, ROPE_HALF), rest[:half_rest],
                           np.arange(ROPE_HALF, ROPE_DIM), rest[half_rest:]])


def _rope_tables():
    inv_freq = ROPE_THETA ** (-jnp.arange(ROPE_HALF, dtype=F32) / ROPE_HALF)
    e = np.zeros((4 * ROPE_HALF, 2 * LANES), np.float32)
    for part in range(2):
        for i in range(ROPE_HALF):
            rc, rs = part * 2 * ROPE_HALF + i, part * 2 * ROPE_HALF + ROPE_HALF + i
            e[rc, i] = 1.0
            e[rc, LANES // 2 + i] = 1.0
            e[rs, LANES + i] = -1.0
            e[rs, LANES + LANES // 2 + i] = 1.0
    return inv_freq.reshape(ROPE_HALF, 1), jnp.asarray(e, BF16)


def kernel(x, c, positions, w_ada, b_ada, g_norm, w_in, b_igate, b_fgate, w_conv, b_conv,
           w_q_mlstm, w_k_mlstm, g_mlstm_head, skip_mlstm, w_out_mlstm, w_out_attn, w_out, g_final):
    B, S, D = x.shape
    assert D == D_MODEL and w_ada.shape[0] == 1
    W = M_WIDTH
    QKV = N_GROUPS * A_GW
    o_gate = 4 * W
    o_q = o_gate + 2 * M_HEADS
    o_k, o_v = o_q + QKV, o_q + 2 * QKV
    o_zb = o_q + 3 * QKV
    o_ga = o_zb + A_GW
    o_gb = o_ga + D
    w = w_in[0]

    mod = _mod_call(c, w_ada[0], b_ada[0], g_norm[0])
    shift_mod = mod[0].reshape(B, 1, D)
    a_mod = mod[1].reshape(B, 1, D)
    gate_mod = mod[2].reshape(B, 1, D)

    w_i = w[:, o_gate:o_gate + M_HEADS]
    w_f = w[:, o_gate + M_HEADS:o_gate + 2 * M_HEADS]
    pad_c = jnp.zeros((D, LANES - M_HEADS), F32)
    wgc = jnp.concatenate([w_i, pad_c, w_f, pad_c], axis=1).astype(BF16)
    pad_r = jnp.zeros((SUBLANES - M_HEADS, D), F32)
    wgr = jnp.concatenate([w_i.T, pad_r, w_f.T, pad_r], axis=0).astype(BF16)
    zc = jnp.zeros((LANES - M_HEADS,), F32)
    bgc = jnp.concatenate([b_igate[0], zc, b_fgate[0], zc]).reshape(1, 2 * LANES)
    zr = jnp.zeros((SUBLANES - M_HEADS,), F32)
    bgr = jnp.concatenate([b_igate[0], zr, b_fgate[0], zr]).reshape(2 * SUBLANES, 1)
    ym, h_nat, h_d4, h_d16 = _mlstm_call(
        x, a_mod, shift_mod, w[:, 0:4 * W].astype(BF16), wgc, wgr, bgc, bgr,
        w_conv[0], b_conv[0].reshape(1, W), w_q_mlstm[0].astype(BF16),
        (w_k_mlstm[0] * (M_HDIM ** -0.5)).astype(BF16),
        g_mlstm_head[0].reshape(1, W), skip_mlstm[0].reshape(1, W),
        w[:, o_ga:o_ga + D].astype(BF16), w_out_mlstm[0].astype(BF16))

    freq, esel = _rope_tables()
    perm = _rope_head_perm()
    os_, ls_ = [], []
    for gi, h_split in enumerate((h_nat.reshape(B, 1, S, D), h_d4, h_d16)):
        assert h_split.shape[1] == ATTN_GROUPS[gi][1]
        cols = (gi * A_GW + np.arange(A_GW).reshape(A_HPG, A_HDIM))
        qk_cols = cols[:, perm].reshape(-1)
        w_qkv = jnp.concatenate([w[:, o_q + qk_cols], w[:, o_k + qk_cols],
                                 w[:, o_v + gi * A_GW:o_v + (gi + 1) * A_GW]], axis=1).astype(BF16)
        o_g, l_g = _attn_call(h_split, positions, freq, esel, w_qkv)
        os_.append(o_g)
        ls_.append(l_g)

    return _final_call(
        x, h_nat, gate_mod, ym, os_, ls_,
        w[:, o_zb:o_zb + A_GW].astype(BF16), w[:, o_gb:o_gb + D].astype(BF16),
        w_out_attn[0].astype(BF16), w_out[0].astype(BF16), g_final.reshape(1, D))
```

```python
import jax
import jax.numpy as jnp
import numpy as np
from jax import lax
from jax.experimental import pallas as pl
from jax.experimental.pallas import tpu as pltpu

F32 = jnp.float32
BF16 = jnp.bfloat16

D_MODEL = 1024
EPS = 1e-6
M_HEADS = 4
M_HDIM = 256
M_WIDTH = M_HEADS * M_HDIM
CONV_W = 4
M_CHUNK = 256
M_STEP = 256
ATTN_GROUPS = ((128, 1), (512, 4), (2048, 16))
N_GROUPS = len(ATTN_GROUPS)
A_HPG = 4
A_HDIM = 128
A_GW = A_HPG * A_HDIM
A_SPAN = 128
ROPE_DIM = A_HDIM // 4
ROPE_HALF = ROPE_DIM // 2
ROPE_THETA = 500000.0
A_TILE = 512
F_TILE = 512
LANES = 128
SUBLANES = 8
VMEM_LIMIT = 56 * 1024 * 1024

assert all(w // d == A_SPAN for w, d in ATTN_GROUPS)


def _sigmoid(x):
    return 1.0 / (1.0 + jnp.exp(-x))


def _silu(x):
    return x * _sigmoid(x)


def _log_sigmoid(x):
    return jnp.minimum(x, 0.0) - jnp.log(1.0 + jnp.exp(-jnp.abs(x)))


def _mod_norm(x, a_row, shift_row):
    ms = jnp.mean(x * x, axis=-1, keepdims=True)
    return x * lax.rsqrt(ms + EPS) * a_row + shift_row


def _dot(a, b):
    return jnp.dot(a, b, preferred_element_type=F32)


def _dot_nt(a, b):
    return lax.dot_general(a, b, (((1,), (1,)), ((), ())), preferred_element_type=F32)


def _dot_tn(a, b):
    return lax.dot_general(a, b, (((0,), (0,)), ((), ())), preferred_element_type=F32)


def _resident(shape):
    zeros = (0,) * len(shape)
    return pl.BlockSpec(shape, lambda *_: zeros, pipeline_mode=pl.Buffered(1))


def _mod_kernel(c_ref, w_ref, b_ref, g_ref, o_ref):
    j = pl.program_id(0)
    val = _dot(_silu(c_ref[...]), w_ref[...]) + b_ref[...]
    o_ref[...] = jnp.where(j == 1, g_ref[...] * (1.0 + val), val)


def _mod_call(c, w_ada, b_ada, g_norm):
    B = c.shape[0]
    return pl.pallas_call(
        _mod_kernel,
        grid=(3,),
        in_specs=[
            pl.BlockSpec((B, D_MODEL), lambda j: (0, 0)),
            pl.BlockSpec((D_MODEL, D_MODEL), lambda j: (0, j)),
            pl.BlockSpec((1, D_MODEL), lambda j: (0, j)),
            pl.BlockSpec((1, D_MODEL), lambda j: (0, 0)),
        ],
        out_specs=pl.BlockSpec((None, B, D_MODEL), lambda j: (j, 0, 0)),
        out_shape=jax.ShapeDtypeStruct((3, B, D_MODEL), F32),
        compiler_params=pltpu.CompilerParams(dimension_semantics=("arbitrary",),
                                             vmem_limit_bytes=VMEM_LIMIT),
        name="adaln_mod",
    )(c, w_ada, b_ada.reshape(1, 3 * D_MODEL), g_norm.reshape(1, D_MODEL))


def _scan_rows(x, op, identity):
    n = x.shape[0]
    row = lax.broadcasted_iota(jnp.int32, x.shape, 0)
    sh = 1
    while sh < n:
        x = op(x, jnp.where(row >= sh, pltpu.roll(x, sh, 0), identity))
        sh *= 2
    return x


def _bf16_parts(x):
    hi = x.astype(BF16).astype(F32)
    mid = (x - hi).astype(BF16).astype(F32)
    return [hi, mid, x - hi - mid]


_N_PARTS = 3
_CUMSUM_ROWS = 4 * SUBLANES
_GATE_STACK_ROWS = 8 * SUBLANES


def _mlstm_kernel(x_ref, a_ref, shift_ref, wm_in, wgr_ref, bgr_ref, tri_ref, gsel_ref, p4_ref,
                  p16_ref, wconv_ref, bconv_ref, wq_ref, wk_ref, ghead_ref, skip_ref, wga_in,
                  wout_in, o_ref, hnat_ref, hd4_ref, hd16_ref, c_scr, n_scr, m_scr, xa_scr, hm_scr,
                  wm_ref, wga_ref, wout_ref):
    L = M_CHUNK
    W = M_WIDTH
    TAIL = SUBLANES

    @pl.when(pl.program_id(1) == 0)
    def _():
        c_scr[...] = jnp.zeros_like(c_scr)
        n_scr[...] = jnp.zeros_like(n_scr)
        m_scr[...] = jnp.zeros_like(m_scr)
        xa_scr[0:TAIL, :] = jnp.zeros((TAIL, W), F32)

    @pl.when((pl.program_id(0) == 0) & (pl.program_id(1) == 0))
    def _():
        wm_ref[...] = wm_in[...]
        wga_ref[...] = wga_in[...]
        wout_ref[...] = wout_in[...]

    causal = (lax.broadcasted_iota(jnp.int32, (L, L), 0) >=
              lax.broadcasted_iota(jnp.int32, (L, L), 1))

    for ci in range(M_STEP // L):
        rows = slice(ci * L, (ci + 1) * L)
        hnat_ref[rows, :] = _mod_norm(x_ref[rows, :], a_ref[...], shift_ref[...]).astype(BF16)
        hb = hnat_ref[rows, :]

        def split_to(p_ref, ref):
            d = ref.shape[0]
            perm = _dot(p_ref[...], hb).astype(BF16)
            for r in range(d):
                ref[r, ci * (L // d):(ci + 1) * (L // d), :] = perm[r * (L // d):(r + 1) * (L // d), :]

        xa_scr[TAIL:TAIL + L, :] = _dot(hb, wm_ref[:, 0:W])
        gr = _dot_nt(wgr_ref[...], hb) + bgr_ref[...]
        split_to(p4_ref, hd4_ref)
        lf = _bf16_parts(_log_sigmoid(gr[SUBLANES:2 * SUBLANES, :]))
        lf.append(jnp.zeros((_CUMSUM_ROWS - _N_PARTS * SUBLANES, L), F32))
        csum = _dot(jnp.concatenate(lf, axis=0).astype(BF16), tri_ref[...])
        split_to(p16_ref, hd16_ref)
        a_row = csum[0:SUBLANES] + csum[SUBLANES:2 * SUBLANES] + csum[2 * SUBLANES:3 * SUBLANES]
        r_row = gr[0:SUBLANES, :] - a_row
        stack = _bf16_parts(r_row) + _bf16_parts(a_row)
        stack.append(jnp.zeros((_GATE_STACK_ROWS - 2 * _N_PARTS * SUBLANES, L), F32))
        rc_ac = _dot_tn(jnp.concatenate(stack, axis=0).astype(BF16), gsel_ref[...])
        ga = _dot(hb, wga_ref[...])

        xcs, vbs = [], []
        for hd in range(M_HEADS):
            cs = slice(hd * M_HDIM, (hd + 1) * M_HDIM)
            conv = bconv_ref[:, cs]
            for j in range(CONV_W):
                off = TAIL - (CONV_W - 1) + j
                conv = conv + wconv_ref[j:j + 1, cs] * xa_scr[off:off + L, cs]
            xcs.append(_silu(conv))
            vbs.append(_dot(hb, wm_ref[:, W + hd * M_HDIM:W + (hd + 1) * M_HDIM]).astype(BF16))
        xa_scr[0:TAIL, :] = xa_scr[L:L + TAIL, :]

        r_col, a_col = rc_ac[:, 0:LANES], rc_ac[:, LANES:2 * LANES]
        m_prev = m_scr[0:1, :]
        big_m = jnp.maximum(_scan_rows(r_col, jnp.maximum, -jnp.inf), m_prev)
        m_last = big_m[L - 1:L, :]
        w_inter_all = jnp.exp(m_prev - big_m)
        floor_all = jnp.exp(-a_col - big_m)
        kscale_all = jnp.exp(r_col - m_last)
        decay_all = jnp.exp(m_prev - m_last)
        m_scr[0:1, :] = a_col[L - 1:L, :] + m_last

        for hd in range(M_HEADS):
            cs = slice(hd * M_HDIM, (hd + 1) * M_HDIM)
            hc = slice(hd, hd + 1)
            xc_h, vb = xcs[hd], vbs[hd]
            xcb = xc_h.astype(BF16)
            qf = _dot(xcb, wq_ref[hd])
            qb = qf.astype(BF16)
            kf = _dot(xcb, wk_ref[hd])
            kb = kf.astype(BF16)

            w_intra = jnp.exp(jnp.where(causal, r_row[hc, :] - big_m[:, hc], -jnp.inf))
            s = _dot_nt(qb, kb) * w_intra
            w_inter = w_inter_all[:, hc]
            c_old = c_scr[hd]
            n_old = n_scr[hc, :]
            num = _dot(s.astype(BF16), vb) + w_inter * _dot(qb, c_old.astype(BF16))
            den = (jnp.sum(s, axis=1, keepdims=True) +
                   w_inter * jnp.sum(qf * n_old, axis=1, keepdims=True))
            hout = num / jnp.maximum(jnp.abs(den), floor_all[:, hc])

            kw = kscale_all[:, hc] * kf
            decay = decay_all[:, hc]
            c_scr[hd] = decay * c_old + _dot_tn(kw.astype(BF16), vb)
            n_scr[hc, :] = decay * n_old + jnp.sum(kw, axis=0, keepdims=True)

            oa = _dot(hb, wm_ref[:, 2 * W + hd * M_HDIM:2 * W + (hd + 1) * M_HDIM])
            hm = _sigmoid(oa) * hout
            hm = hm * lax.rsqrt(jnp.mean(hm * hm, axis=-1, keepdims=True) + EPS) * ghead_ref[:, cs]
            hm = hm + skip_ref[:, cs] * xc_h
            za = _dot(hb, wm_ref[:, 3 * W + hd * M_HDIM:3 * W + (hd + 1) * M_HDIM])
            hm_scr[rows, cs] = (hm * _silu(za)).astype(BF16)

        y_m = _dot(hm_scr[rows, :], wout_ref[...])
        o_ref[rows, :] = (_sigmoid(ga) * y_m).astype(o_ref.dtype)


def _mlstm_call(x, a_mod, shift_mod, wm, wgr, bgr, tri, gsel, p4, p16, wconv, bconv, wq, wk, ghead, skip,
                wga, wout):
    B, S, D = x.shape
    L, T = M_CHUNK, M_STEP
    d4, d16 = ATTN_GROUPS[1][1], ATTN_GROUPS[2][1]
    assert S % T == 0 and T % L == 0 and L % (d16 * 2 * SUBLANES) == 0
    tok = pl.BlockSpec((None, T, D), lambda b, c: (b, c, 0))
    per_batch = pl.BlockSpec((None, 1, D), lambda b, c: (b, 0, 0))
    split = lambda d: pl.BlockSpec((None, d, T // d, D), lambda b, c: (b, 0, c, 0))
    args = (wm, wgr, bgr, tri, gsel, p4, p16, wconv, bconv, wq, wk, ghead, skip, wga, wout)
    return pl.pallas_call(
        _mlstm_kernel,
        grid=(B, S // T),
        in_specs=[tok, per_batch, per_batch] + [_resident(a.shape) for a in args],
        out_specs=[tok, tok, split(d4), split(d16)],
        out_shape=[
            jax.ShapeDtypeStruct((B, S, D), BF16),
            jax.ShapeDtypeStruct((B, S, D), BF16),
            jax.ShapeDtypeStruct((B, d4, S // d4, D), BF16),
            jax.ShapeDtypeStruct((B, d16, S // d16, D), BF16),
        ],
        scratch_shapes=[
            pltpu.VMEM((M_HEADS, M_HDIM, M_HDIM), F32),
            pltpu.VMEM((SUBLANES, M_HDIM), F32),
            pltpu.VMEM((SUBLANES, LANES), F32),
            pltpu.VMEM((L + SUBLANES, M_WIDTH), F32),
            pltpu.VMEM((T, M_WIDTH), BF16),
            pltpu.VMEM(wm.shape, BF16),
            pltpu.VMEM(wga.shape, BF16),
            pltpu.VMEM(wout.shape, BF16),
        ],
        compiler_params=pltpu.CompilerParams(dimension_semantics=("arbitrary", "arbitrary"),
                                             vmem_limit_bytes=VMEM_LIMIT),
        name="mlstm_branch",
    )(x, a_mod, shift_mod, *args)


def _attn_kernel(h_ref, pos_ref, freq_ref, esel_ref, w_in, o_ref, lse_ref, k_scr, v_scr, w_ref):
    n = h_ref.shape[0]
    ub = pl.program_id(2)

    @pl.when((pl.program_id(0) == 0) & (pl.program_id(1) == 0) & (ub == 0))
    def _():
        w_ref[...] = w_in[...]

    @pl.when(ub == 0)
    def _():
        k_scr[0:A_SPAN, :] = jnp.zeros((A_SPAN, A_GW), BF16)
        v_scr[0:A_SPAN, :] = jnp.zeros((A_SPAN, A_GW), BF16)

    @pl.when(ub > 0)
    def _():
        k_scr[0:A_SPAN, :] = k_scr[n:n + A_SPAN, :]
        v_scr[0:A_SPAN, :] = v_scr[n:n + A_SPAN, :]

    hb = h_ref[...]

    ang = freq_ref[...] * pos_ref[...].astype(F32)
    cs = jnp.concatenate([jnp.cos(ang), jnp.sin(ang)], axis=0)
    cs_hi = cs.astype(BF16)
    cs_lo = (cs - cs_hi.astype(F32)).astype(BF16)
    tab = _dot_tn(jnp.concatenate([cs_hi, cs_lo], axis=0), esel_ref[...])
    lane = lax.broadcasted_iota(jnp.int32, (1, LANES), 1)
    rot_lane = (lane % (LANES // 2)) < ROPE_HALF
    cos_t = tab[:, 0:LANES] + jnp.where(rot_lane, 0.0, 1.0)
    sin_t = tab[:, LANES:2 * LANES]

    def rope(t):
        return t * cos_t + pltpu.roll(t, LANES // 2, 1) * sin_t

    scale = A_HDIM ** -0.5
    q = _dot(hb, w_ref[:, 0:A_GW])
    k = _dot(hb, w_ref[:, A_GW:2 * A_GW])
    qs = []
    for hd in range(A_HPG):
        cs_ = slice(hd * A_HDIM, (hd + 1) * A_HDIM)
        qs.append((rope(q[:, cs_]) * scale).astype(BF16))
        k_scr[A_SPAN:A_SPAN + n, cs_] = rope(k[:, cs_]).astype(BF16)
    v_scr[A_SPAN:A_SPAN + n, :] = _dot(hb, w_ref[:, 2 * A_GW:3 * A_GW]).astype(BF16)

    p_idx = lax.broadcasted_iota(jnp.int32, (A_SPAN, 2 * A_SPAN), 0)
    c_idx = lax.broadcasted_iota(jnp.int32, (A_SPAN, 2 * A_SPAN), 1)
    band = (c_idx >= p_idx) & (c_idx <= p_idx + A_SPAN)
    lane_o = lax.broadcasted_iota(jnp.int32, (A_SPAN, LANES), 1)
    for jb in range(n // A_SPAN):
        rows = slice(jb * A_SPAN, (jb + 1) * A_SPAN)
        krows = slice(jb * A_SPAN, (jb + 2) * A_SPAN)
        if jb == 0:
            valid = band & (c_idx >= jnp.where(ub == 0, A_SPAN, 0))
        else:
            valid = band
        lse_blk = jnp.zeros((A_SPAN, LANES), F32)
        for hd in range(A_HPG):
            cs_ = slice(hd * A_HDIM, (hd + 1) * A_HDIM)
            s = _dot_nt(qs[hd][rows, :], k_scr[krows, cs_])
            s = jnp.where(valid, s, -jnp.inf)
            mx = jnp.max(s, axis=1, keepdims=True)
            e = jnp.exp(s - mx)
            den = jnp.sum(e, axis=1, keepdims=True)
            o = _dot(e.astype(BF16), v_scr[krows, cs_]) / den
            o_ref[rows, cs_] = o.astype(o_ref.dtype)
            lse_blk = jnp.where(lane_o == hd, mx + jnp.log(den), lse_blk)
        lse_ref[rows, :] = lse_blk


def _attn_call(h_split, positions, freq, esel, w_qkv):
    B, d, U, D = h_split.shape
    n = min(A_TILE, U)
    assert U % n == 0 and n % A_SPAN == 0
    pos = positions.reshape(B, U, d).transpose(0, 2, 1).reshape(B, d, 1, U)
    tok = lambda w: pl.BlockSpec((None, None, n, w), lambda b, r, u: (b, r, u, 0))
    return pl.pallas_call(
        _attn_kernel,
        grid=(B, d, U // n),
        in_specs=[
            tok(D),
            pl.BlockSpec((None, None, 1, n), lambda b, r, u: (b, r, 0, u)),
            _resident(freq.shape), _resident(esel.shape), _resident(w_qkv.shape),
        ],
        out_specs=[tok(A_GW), tok(LANES)],
        out_shape=[
            jax.ShapeDtypeStruct((B, d, U, A_GW), BF16),
            jax.ShapeDtypeStruct((B, d, U, LANES), F32),
        ],
        scratch_shapes=[
            pltpu.VMEM((A_SPAN + n, A_GW), BF16),
            pltpu.VMEM((A_SPAN + n, A_GW), BF16),
            pltpu.VMEM(w_qkv.shape, BF16),
        ],
        compiler_params=pltpu.CompilerParams(
            dimension_semantics=("arbitrary", "arbitrary", "arbitrary"),
            vmem_limit_bytes=VMEM_LIMIT),
        name=f"dilated_attn_d{d}",
    )(h_split, pos, freq, esel, w_qkv)


def _final_kernel(x_ref, h_ref, gate_ref, ym_ref, o1_ref, o2_ref, o3_ref, l1_ref, l2_ref, l3_ref,
                  wz_in, wgb_in, woa_in, wout_in, gfin_ref, out_ref, oz_scr, o_scr, l_scr,
                  wz_ref, wgb_ref, woa_ref, wout_ref):
    T = F_TILE

    @pl.when((pl.program_id(0) == 0) & (pl.program_id(1) == 0))
    def _():
        wz_ref[...] = wz_in[...]
        wgb_ref[...] = wgb_in[...]
        woa_ref[...] = woa_in[...]
        wout_ref[...] = wout_in[...]

    x = x_ref[...]
    hb = h_ref[...]

    for gi, (o_ref, l_ref) in enumerate(((o1_ref, l1_ref), (o2_ref, l2_ref), (o3_ref, l3_ref))):
        d = ATTN_GROUPS[gi][1]
        for r in range(d):
            rows = pl.ds(r, T // d, stride=d)
            l_scr[gi, rows, :] = l_ref[r]
            for hd in range(A_HPG):
                o_scr[gi * A_HPG + hd, rows, :] = (
                    o_ref[r, :, hd * A_HDIM:(hd + 1) * A_HDIM].astype(F32))

    l1, l2, l3 = l_scr[0], l_scr[1], l_scr[2]
    mx = jnp.maximum(jnp.maximum(l1, l2), l3)
    e1, e2, e3 = jnp.exp(l1 - mx), jnp.exp(l2 - mx), jnp.exp(l3 - mx)
    inv = 1.0 / (e1 + e2 + e3)
    w1, w2, w3 = e1 * inv, e2 * inv, e3 * inv

    zb = _dot(hb, wz_ref[...])
    for hd in range(A_HPG):
        cs = slice(hd * A_HDIM, (hd + 1) * A_HDIM)
        o_a = (w1[:, hd:hd + 1] * o_scr[hd] + w2[:, hd:hd + 1] * o_scr[A_HPG + hd] +
               w3[:, hd:hd + 1] * o_scr[2 * A_HPG + hd])
        oz_scr[:, cs] = (o_a * _silu(zb[:, cs])).astype(BF16)
    y_a = _dot(oz_scr[...], woa_ref[...])
    gb = _dot(hb, wgb_ref[...])
    merged = ym_ref[...].astype(F32) + _sigmoid(gb) * y_a
    xn = x + gate_ref[...] * _dot(merged.astype(BF16), wout_ref[...])
    out_ref[...] = xn * lax.rsqrt(jnp.mean(xn * xn, axis=-1, keepdims=True) + EPS) * gfin_ref[...]


def _final_call(x, h_nat, gate_mod, ym, os_, ls_, wz, wgb, woa, wout, gfin):
    B, S, D = x.shape
    T = F_TILE
    assert S % T == 0 and all(T % (d * 2 * SUBLANES) == 0 for _, d in ATTN_GROUPS)
    tok = lambda w: pl.BlockSpec((None, T, w), lambda b, t: (b, t, 0))
    split = lambda a: pl.BlockSpec((None, a.shape[1], T // a.shape[1], a.shape[3]),
                                   lambda b, t: (b, 0, t, 0))
    per_batch = pl.BlockSpec((None, 1, D), lambda b, t: (b, 0, 0))
    wts = (wz, wgb, woa, wout, gfin)
    return pl.pallas_call(
        _final_kernel,
        grid=(B, S // T),
        in_specs=([tok(D), tok(D), per_batch, tok(D)] + [split(a) for a in os_] +
                  [split(a) for a in ls_] + [_resident(a.shape) for a in wts]),
        out_specs=tok(D),
        out_shape=jax.ShapeDtypeStruct((B, S, D), F32),
        scratch_shapes=[
            pltpu.VMEM((T, A_GW), BF16),
            pltpu.VMEM((N_GROUPS * A_HPG, T, A_HDIM), F32),
            pltpu.VMEM((N_GROUPS, T, LANES), F32),
            pltpu.VMEM(wz.shape, BF16),
            pltpu.VMEM(wgb.shape, BF16),
            pltpu.VMEM(woa.shape, BF16),
            pltpu.VMEM(wout.shape, BF16),
        ],
        compiler_params=pltpu.CompilerParams(dimension_semantics=("arbitrary", "arbitrary"),
                                             vmem_limit_bytes=VMEM_LIMIT),
        name="merge_out_norm",
    )(x, h_nat, gate_mod, ym, *os_, *ls_, *wts)


def _rope_head_perm():
    rest = np.arange(ROPE_DIM, A_HDIM)
    half_rest = (A_HDIM // 2) - ROPE_HALF
    return np.concatenate([np.arange(0, ROPE_HALF), rest[:half_rest],
                           np.arange(ROPE_HALF, ROPE_DIM), rest[half_rest:]])


def _rope_tables():
    inv_freq = ROPE_THETA ** (-jnp.arange(ROPE_HALF, dtype=F32) / ROPE_HALF)
    e = np.zeros((4 * ROPE_HALF, 2 * LANES), np.float32)
    for part in range(2):
        for i in range(ROPE_HALF):
            rc, rs = part * 2 * ROPE_HALF + i, part * 2 * ROPE_HALF + ROPE_HALF + i
            e[rc, i] = 1.0
            e[rc, LANES // 2 + i] = 1.0
            e[rs, LANES + i] = -1.0
            e[rs, LANES + LANES // 2 + i] = 1.0
    return inv_freq.reshape(ROPE_HALF, 1), jnp.asarray(e, BF16)


def _gate_select():
    e = np.zeros((_GATE_STACK_ROWS, 2 * LANES), np.float32)
    for q in range(2):
        for part in range(_N_PARTS):
            for hd in range(M_HEADS):
                e[(q * _N_PARTS + part) * SUBLANES + hd, q * LANES + hd] = 1.0
    return jnp.asarray(e, BF16)


def _cumsum_triangle():
    return jnp.asarray(np.triu(np.ones((M_CHUNK, M_CHUNK), np.float32)), BF16)


def _split_perm(d):
    p = np.zeros((M_CHUNK, M_CHUNK), np.float32)
    out = np.arange(M_CHUNK)
    p[out, (out % (M_CHUNK // d)) * d + out // (M_CHUNK // d)] = 1.0
    return jnp.asarray(p, BF16)


def kernel(x, c, positions, w_ada, b_ada, g_norm, w_in, b_igate, b_fgate, w_conv, b_conv,
           w_q_mlstm, w_k_mlstm, g_mlstm_head, skip_mlstm, w_out_mlstm, w_out_attn, w_out, g_final):
    B, S, D = x.shape
    assert D == D_MODEL and w_ada.shape[0] == 1
    W = M_WIDTH
    QKV = N_GROUPS * A_GW
    o_gate = 4 * W
    o_q = o_gate + 2 * M_HEADS
    o_k, o_v = o_q + QKV, o_q + 2 * QKV
    o_zb = o_q + 3 * QKV
    o_ga = o_zb + A_GW
    o_gb = o_ga + D
    w = w_in[0]

    mod = _mod_call(c, w_ada[0], b_ada[0], g_norm[0])
    shift_mod = mod[0].reshape(B, 1, D)
    a_mod = mod[1].reshape(B, 1, D)
    gate_mod = mod[2].reshape(B, 1, D)

    w_i = w[:, o_gate:o_gate + M_HEADS]
    w_f = w[:, o_gate + M_HEADS:o_gate + 2 * M_HEADS]
    pad_r = jnp.zeros((SUBLANES - M_HEADS, D), F32)
    wgr = jnp.concatenate([w_i.T, pad_r, w_f.T, pad_r], axis=0).astype(BF16)
    zr = jnp.zeros((SUBLANES - M_HEADS,), F32)
    bgr = jnp.concatenate([b_igate[0], zr, b_fgate[0], zr]).reshape(2 * SUBLANES, 1)
    ym, h_nat, h_d4, h_d16 = _mlstm_call(
        x, a_mod, shift_mod, w[:, 0:4 * W].astype(BF16), wgr, bgr, _cumsum_triangle(), _gate_select(),
        _split_perm(ATTN_GROUPS[1][1]), _split_perm(ATTN_GROUPS[2][1]),
        w_conv[0], b_conv[0].reshape(1, W), w_q_mlstm[0].astype(BF16),
        (w_k_mlstm[0] * (M_HDIM ** -0.5)).astype(BF16),
        g_mlstm_head[0].reshape(1, W), skip_mlstm[0].reshape(1, W),
        w[:, o_ga:o_ga + D].astype(BF16), w_out_mlstm[0].astype(BF16))

    freq, esel = _rope_tables()
    perm = _rope_head_perm()
    os_, ls_ = [], []
    for gi, h_split in enumerate((h_nat.reshape(B, 1, S, D), h_d4, h_d16)):
        assert h_split.shape[1] == ATTN_GROUPS[gi][1]
        cols = (gi * A_GW + np.arange(A_GW).reshape(A_HPG, A_HDIM))
        qk_cols = cols[:, perm].reshape(-1)
        w_qkv = jnp.concatenate([w[:, o_q + qk_cols], w[:, o_k + qk_cols],
                                 w[:, o_v + gi * A_GW:o_v + (gi + 1) * A_GW]], axis=1).astype(BF16)
        o_g, l_g = _attn_call(h_split, positions, freq, esel, w_qkv)
        os_.append(o_g)
        ls_.append(l_g)

    return _final_call(
        x, h_nat, gate_mod, ym, os_, ls_,
        w[:, o_zb:o_zb + A_GW].astype(BF16), w[:, o_gb:o_gb + D].astype(BF16),
        w_out_attn[0].astype(BF16), w_out[0].astype(BF16), g_final.reshape(1, D))
```

```python
import jax
import jax.numpy as jnp
import numpy as np
from jax import lax
from jax.experimental import pallas as pl
from jax.experimental.pallas import tpu as pltpu

F32 = jnp.float32
BF16 = jnp.bfloat16

D_MODEL = 1024
EPS = 1e-6
M_HEADS = 4
M_HDIM = 256
M_WIDTH = M_HEADS * M_HDIM
CONV_W = 4
M_CHUNK = 256
M_STEP = 256
ATTN_GROUPS = ((128, 1), (512, 4), (2048, 16))
N_GROUPS = len(ATTN_GROUPS)
A_HPG = 4
A_HDIM = 128
A_GW = A_HPG * A_HDIM
A_SPAN = 128
ROPE_DIM = A_HDIM // 4
ROPE_HALF = ROPE_DIM // 2
ROPE_THETA = 500000.0
A_TILE = 512
F_TILE = 512
F_SUB = 256
LANES = 128
SUBLANES = 8
VMEM_LIMIT = 56 * 1024 * 1024

assert all(w // d == A_SPAN for w, d in ATTN_GROUPS)
_ROPE_PIECES = ((0, ROPE_HALF), (ROPE_DIM, LANES // 2 + ROPE_HALF), (ROPE_HALF, ROPE_DIM),
                (LANES // 2 + ROPE_HALF, A_HDIM))


def _sigmoid(x):
    return 1.0 / (1.0 + jnp.exp(-x))


def _silu(x):
    return x * _sigmoid(x)


def _log_sigmoid(x):
    return jnp.minimum(x, 0.0) - jnp.log(1.0 + jnp.exp(-jnp.abs(x)))


def _mod_norm(x, a_row, shift_row):
    ms = jnp.mean(x * x, axis=-1, keepdims=True)
    return x * lax.rsqrt(ms + EPS) * a_row + shift_row


def _dot(a, b):
    return jnp.dot(a, b, preferred_element_type=F32)


def _dot_nt(a, b):
    return lax.dot_general(a, b, (((1,), (1,)), ((), ())), preferred_element_type=F32)


def _dot_tn(a, b):
    return lax.dot_general(a, b, (((0,), (0,)), ((), ())), preferred_element_type=F32)


def _resident(shape):
    zeros = (0,) * len(shape)
    return pl.BlockSpec(shape, lambda *_: zeros, pipeline_mode=pl.Buffered(1))


def _mod_kernel(c_ref, w_ref, b_ref, g_ref, o_ref):
    j = pl.program_id(0)
    val = _dot(_silu(c_ref[...]), w_ref[...]) + b_ref[...]
    o_ref[...] = jnp.where(j == 1, g_ref[...] * (1.0 + val), val)


def _mod_call(c, w_ada, b_ada, g_norm):
    B = c.shape[0]
    return pl.pallas_call(
        _mod_kernel,
        grid=(3,),
        in_specs=[
            pl.BlockSpec((B, D_MODEL), lambda j: (0, 0)),
            pl.BlockSpec((D_MODEL, D_MODEL), lambda j: (0, j)),
            pl.BlockSpec((1, D_MODEL), lambda j: (0, j)),
            pl.BlockSpec((1, D_MODEL), lambda j: (0, 0)),
        ],
        out_specs=pl.BlockSpec((None, B, D_MODEL), lambda j: (j, 0, 0)),
        out_shape=jax.ShapeDtypeStruct((3, B, D_MODEL), F32),
        compiler_params=pltpu.CompilerParams(dimension_semantics=("arbitrary",),
                                             vmem_limit_bytes=VMEM_LIMIT),
        name="adaln_mod",
    )(c, w_ada, b_ada.reshape(1, 3 * D_MODEL), g_norm.reshape(1, D_MODEL))


def _scan_rows(x, op, identity):
    n = x.shape[0]
    row = lax.broadcasted_iota(jnp.int32, x.shape, 0)
    sh = 1
    while sh < n:
        x = op(x, jnp.where(row >= sh, pltpu.roll(x, sh, 0), identity))
        sh *= 2
    return x


def _bf16_parts(x):
    hi = x.astype(BF16).astype(F32)
    mid = (x - hi).astype(BF16).astype(F32)
    return [hi, mid, x - hi - mid]


_N_PARTS = 3
_CUMSUM_ROWS = 4 * SUBLANES
_GATE_STACK_ROWS = 8 * SUBLANES


def _mlstm_kernel(x_ref, a_ref, shift_ref, wm_in, wgr_ref, bgr_ref, tri_ref, gsel_ref, p4_ref,
                  p16_ref, wconv_ref, bconv_ref, wq_ref, wk_ref, ghead_ref, skip_ref, wga_in,
                  wout_in, o_ref, hnat_ref, hd4_ref, hd16_ref, c_scr, n_scr, m_scr, xa_scr, hm_scr,
                  wm_ref, wga_ref, wout_ref):
    L = M_CHUNK
    W = M_WIDTH
    TAIL = SUBLANES

    @pl.when(pl.program_id(1) == 0)
    def _():
        c_scr[...] = jnp.zeros_like(c_scr)
        n_scr[...] = jnp.zeros_like(n_scr)
        m_scr[...] = jnp.zeros_like(m_scr)
        xa_scr[0:TAIL, :] = jnp.zeros((TAIL, W), F32)

    @pl.when((pl.program_id(0) == 0) & (pl.program_id(1) == 0))
    def _():
        wm_ref[...] = wm_in[...]
        wga_ref[...] = wga_in[...]
        wout_ref[...] = wout_in[...]

    causal = (lax.broadcasted_iota(jnp.int32, (L, L), 0) >=
              lax.broadcasted_iota(jnp.int32, (L, L), 1))

    for ci in range(M_STEP // L):
        rows = slice(ci * L, (ci + 1) * L)
        hnat_ref[rows, :] = _mod_norm(x_ref[rows, :], a_ref[...], shift_ref[...]).astype(BF16)
        hb = hnat_ref[rows, :]

        def split_to(p_ref, ref):
            d = ref.shape[0]
            perm = _dot(p_ref[...], hb).astype(BF16)
            for r in range(d):
                ref[r, ci * (L // d):(ci + 1) * (L // d), :] = perm[r * (L // d):(r + 1) * (L // d), :]

        xa_scr[TAIL:TAIL + L, :] = _dot(hb, wm_ref[:, 0:W])
        gr = _dot_nt(wgr_ref[...], hb) + bgr_ref[...]
        split_to(p4_ref, hd4_ref)
        lf = _bf16_parts(_log_sigmoid(gr[SUBLANES:2 * SUBLANES, :]))
        lf.append(jnp.zeros((_CUMSUM_ROWS - _N_PARTS * SUBLANES, L), F32))
        csum = _dot(jnp.concatenate(lf, axis=0).astype(BF16), tri_ref[...])
        split_to(p16_ref, hd16_ref)
        a_row = csum[0:SUBLANES] + csum[SUBLANES:2 * SUBLANES] + csum[2 * SUBLANES:3 * SUBLANES]
        r_row = gr[0:SUBLANES, :] - a_row
        stack = _bf16_parts(r_row) + _bf16_parts(a_row)
        stack.append(jnp.zeros((_GATE_STACK_ROWS - 2 * _N_PARTS * SUBLANES, L), F32))
        rc_ac = _dot_tn(jnp.concatenate(stack, axis=0).astype(BF16), gsel_ref[...])
        ga = _dot(hb, wga_ref[...])

        xcs, vbs = [], []
        for hd in range(M_HEADS):
            cs = slice(hd * M_HDIM, (hd + 1) * M_HDIM)
            conv = bconv_ref[:, cs]
            for j in range(CONV_W):
                off = TAIL - (CONV_W - 1) + j
                conv = conv + wconv_ref[j:j + 1, cs] * xa_scr[off:off + L, cs]
            xcs.append(_silu(conv))
            vbs.append(_dot(hb, wm_ref[:, W + hd * M_HDIM:W + (hd + 1) * M_HDIM]).astype(BF16))
        xa_scr[0:TAIL, :] = xa_scr[L:L + TAIL, :]

        r_col, a_col = rc_ac[:, 0:LANES], rc_ac[:, LANES:2 * LANES]
        m_prev = m_scr[0:1, :]
        big_m = jnp.maximum(_scan_rows(r_col, jnp.maximum, -jnp.inf), m_prev)
        m_last = big_m[L - 1:L, :]
        w_inter_all = jnp.exp(m_prev - big_m)
        floor_all = jnp.exp(-a_col - big_m)
        kscale_all = jnp.exp(r_col - m_last)
        decay_all = jnp.exp(m_prev - m_last)
        m_scr[0:1, :] = a_col[L - 1:L, :] + m_last

        for hd in range(M_HEADS):
            cs = slice(hd * M_HDIM, (hd + 1) * M_HDIM)
            hc = slice(hd, hd + 1)
            xc_h, vb = xcs[hd], vbs[hd]
            xcb = xc_h.astype(BF16)
            qf = _dot(xcb, wq_ref[hd])
            qb = qf.astype(BF16)
            kf = _dot(xcb, wk_ref[hd])
            kb = kf.astype(BF16)

            w_intra = jnp.exp(jnp.where(causal, r_row[hc, :] - big_m[:, hc], -jnp.inf))
            s = _dot_nt(qb, kb) * w_intra
            w_inter = w_inter_all[:, hc]
            c_old = c_scr[hd]
            n_old = n_scr[hc, :]
            num = _dot(s.astype(BF16), vb) + w_inter * _dot(qb, c_old.astype(BF16))
            den = (jnp.sum(s, axis=1, keepdims=True) +
                   w_inter * jnp.sum(qf * n_old, axis=1, keepdims=True))
            hout = num / jnp.maximum(jnp.abs(den), floor_all[:, hc])

            kw = kscale_all[:, hc] * kf
            decay = decay_all[:, hc]
            c_scr[hd] = decay * c_old + _dot_tn(kw.astype(BF16), vb)
            n_scr[hc, :] = decay * n_old + jnp.sum(kw, axis=0, keepdims=True)

            oa = _dot(hb, wm_ref[:, 2 * W + hd * M_HDIM:2 * W + (hd + 1) * M_HDIM])
            hm = _sigmoid(oa) * hout
            hm = hm * lax.rsqrt(jnp.mean(hm * hm, axis=-1, keepdims=True) + EPS) * ghead_ref[:, cs]
            hm = hm + skip_ref[:, cs] * xc_h
            za = _dot(hb, wm_ref[:, 3 * W + hd * M_HDIM:3 * W + (hd + 1) * M_HDIM])
            hm_scr[rows, cs] = (hm * _silu(za)).astype(BF16)

        y_m = _dot(hm_scr[rows, :], wout_ref[...])
        o_ref[rows, :] = (_sigmoid(ga) * y_m).astype(o_ref.dtype)


def _mlstm_call(x, a_mod, shift_mod, wm, wgr, bgr, tri, gsel, p4, p16, wconv, bconv, wq, wk, ghead, skip,
                wga, wout):
    B, S, D = x.shape
    L, T = M_CHUNK, M_STEP
    d4, d16 = ATTN_GROUPS[1][1], ATTN_GROUPS[2][1]
    assert S % T == 0 and T % L == 0 and L % (d16 * 2 * SUBLANES) == 0
    tok = pl.BlockSpec((None, T, D), lambda b, c: (b, c, 0))
    per_batch = pl.BlockSpec((None, 1, D), lambda b, c: (b, 0, 0))
    split = lambda d: pl.BlockSpec((None, d, T // d, D), lambda b, c: (b, 0, c, 0))
    args = (wm, wgr, bgr, tri, gsel, p4, p16, wconv, bconv, wq, wk, ghead, skip, wga, wout)
    return pl.pallas_call(
        _mlstm_kernel,
        grid=(B, S // T),
        in_specs=[tok, per_batch, per_batch] + [_resident(a.shape) for a in args],
        out_specs=[tok, tok, split(d4), split(d16)],
        out_shape=[
            jax.ShapeDtypeStruct((B, S, D), BF16),
            jax.ShapeDtypeStruct((B, S, D), BF16),
            jax.ShapeDtypeStruct((B, d4, S // d4, D), BF16),
            jax.ShapeDtypeStruct((B, d16, S // d16, D), BF16),
        ],
        scratch_shapes=[
            pltpu.VMEM((M_HEADS, M_HDIM, M_HDIM), F32),
            pltpu.VMEM((SUBLANES, M_HDIM), F32),
            pltpu.VMEM((SUBLANES, LANES), F32),
            pltpu.VMEM((L + SUBLANES, M_WIDTH), F32),
            pltpu.VMEM((T, M_WIDTH), BF16),
            pltpu.VMEM(wm.shape, BF16),
            pltpu.VMEM(wga.shape, BF16),
            pltpu.VMEM(wout.shape, BF16),
        ],
        compiler_params=pltpu.CompilerParams(dimension_semantics=("arbitrary", "arbitrary"),
                                             vmem_limit_bytes=VMEM_LIMIT),
        name="mlstm_branch",
    )(x, a_mod, shift_mod, *args)


def _attn_kernel(h_ref, pos_ref, freq_ref, esel_ref, w_in, o_ref, lse_ref, k_scr, v_scr, w_ref):
    n_res, n = h_ref.shape[0], h_ref.shape[1]
    ub = pl.program_id(2)

    @pl.when((pl.program_id(0) == 0) & (pl.program_id(1) == 0) & (ub == 0))
    def _():
        for blk in range(2 * A_HPG):
            cs_ = slice(blk * A_HDIM, (blk + 1) * A_HDIM)
            src = w_in[:, cs_]
            w_ref[:, cs_] = jnp.concatenate([src[:, lo:hi] for lo, hi in _ROPE_PIECES],
                                            axis=1).astype(BF16)
        w_ref[:, 2 * A_GW:3 * A_GW] = w_in[:, 2 * A_GW:3 * A_GW].astype(BF16)

    @pl.when(ub == 0)
    def _():
        k_scr[:, 0:A_SPAN, :] = jnp.zeros((n_res, A_SPAN, A_GW), BF16)
        v_scr[:, 0:A_SPAN, :] = jnp.zeros((n_res, A_SPAN, A_GW), BF16)

    @pl.when(ub > 0)
    def _():
        k_scr[:, 0:A_SPAN, :] = k_scr[:, n:n + A_SPAN, :]
        v_scr[:, 0:A_SPAN, :] = v_scr[:, n:n + A_SPAN, :]

    lane = lax.broadcasted_iota(jnp.int32, (1, LANES), 1)
    rot_lane = (lane % (LANES // 2)) < ROPE_HALF
    p_idx = lax.broadcasted_iota(jnp.int32, (A_SPAN, 2 * A_SPAN), 0)
    c_idx = lax.broadcasted_iota(jnp.int32, (A_SPAN, 2 * A_SPAN), 1)
    band = (c_idx >= p_idx) & (c_idx <= p_idx + A_SPAN)
    lane_o = lax.broadcasted_iota(jnp.int32, (A_SPAN, LANES), 1)
    scale = A_HDIM ** -0.5

    for rr in range(n_res):
        hb = h_ref[rr]

        ang = freq_ref[...] * pos_ref[rr].astype(F32)
        cs = jnp.concatenate([jnp.cos(ang), jnp.sin(ang)], axis=0)
        cs_hi = cs.astype(BF16)
        cs_lo = (cs - cs_hi.astype(F32)).astype(BF16)
        tab = _dot_tn(jnp.concatenate([cs_hi, cs_lo], axis=0), esel_ref[...])
        cos_t = tab[:, 0:LANES] + jnp.where(rot_lane, 0.0, 1.0)
        sin_t = tab[:, LANES:2 * LANES]

        def rope(t):
            return t * cos_t + pltpu.roll(t, LANES // 2, 1) * sin_t

        q = _dot(hb, w_ref[:, 0:A_GW])
        k = _dot(hb, w_ref[:, A_GW:2 * A_GW])
        qs = []
        for hd in range(A_HPG):
            cs_ = slice(hd * A_HDIM, (hd + 1) * A_HDIM)
            qs.append((rope(q[:, cs_]) * scale).astype(BF16))
            k_scr[rr, A_SPAN:A_SPAN + n, cs_] = rope(k[:, cs_]).astype(BF16)
        v_scr[rr, A_SPAN:A_SPAN + n, :] = _dot(hb, w_ref[:, 2 * A_GW:3 * A_GW]).astype(BF16)

        for jb in range(n // A_SPAN):
            rows = slice(jb * A_SPAN, (jb + 1) * A_SPAN)
            krows = slice(jb * A_SPAN, (jb + 2) * A_SPAN)
            if jb == 0:
                valid = band & (c_idx >= jnp.where(ub == 0, A_SPAN, 0))
            else:
                valid = band
            lse_blk = jnp.zeros((A_SPAN, LANES), F32)
            for hd in range(A_HPG):
                cs_ = slice(hd * A_HDIM, (hd + 1) * A_HDIM)
                s = _dot_nt(qs[hd][rows, :], k_scr[rr, krows, cs_])
                s = jnp.where(valid, s, -jnp.inf)
                mx = jnp.max(s, axis=1, keepdims=True)
                e = jnp.exp(s - mx)
                den = jnp.sum(e, axis=1, keepdims=True)
                o = _dot(e.astype(BF16), v_scr[rr, krows, cs_]) / den
                o_ref[rr, rows, cs_] = o.astype(o_ref.dtype)
                lse_blk = jnp.where(lane_o == hd, mx + jnp.log(den), lse_blk)
            lse_ref[rr, rows, :] = lse_blk


def _attn_call(h_split, positions, freq, esel, w_qkv):
    B, d, U, D = h_split.shape
    n = min(A_TILE, U)
    n_res = min(d, A_TILE // n)
    assert U % n == 0 and n % A_SPAN == 0 and d % n_res == 0
    pos = positions.reshape(B, U, d).transpose(0, 2, 1).reshape(B, d, 1, U)
    tok = lambda w: pl.BlockSpec((None, n_res, n, w), lambda b, r, u: (b, r, u, 0))
    return pl.pallas_call(
        _attn_kernel,
        grid=(B, d // n_res, U // n),
        in_specs=[
            tok(D),
            pl.BlockSpec((None, n_res, 1, n), lambda b, r, u: (b, r, 0, u)),
            _resident(freq.shape), _resident(esel.shape), _resident(w_qkv.shape),
        ],
        out_specs=[tok(A_GW), tok(LANES)],
        out_shape=[
            jax.ShapeDtypeStruct((B, d, U, A_GW), BF16),
            jax.ShapeDtypeStruct((B, d, U, LANES), F32),
        ],
        scratch_shapes=[
            pltpu.VMEM((n_res, A_SPAN + n, A_GW), BF16),
            pltpu.VMEM((n_res, A_SPAN + n, A_GW), BF16),
            pltpu.VMEM(w_qkv.shape, BF16),
        ],
        compiler_params=pltpu.CompilerParams(
            dimension_semantics=("arbitrary", "arbitrary", "arbitrary"),
            vmem_limit_bytes=VMEM_LIMIT),
        name=f"dilated_attn_d{d}",
    )(h_split, pos, freq, esel, w_qkv)


def _final_kernel(x_ref, h_ref, gate_ref, ym_ref, o1_ref, o2_ref, o3_ref, l1_ref, l2_ref, l3_ref,
                  wz_in, wgb_in, woa_in, wout_in, gfin_ref, out_ref, oz_scr, o_scr, l_scr,
                  wz_ref, wgb_ref, woa_ref, wout_ref):
    T = F_TILE

    @pl.when((pl.program_id(0) == 0) & (pl.program_id(1) == 0))
    def _():
        wz_ref[...] = wz_in[...]
        wgb_ref[...] = wgb_in[...]
        woa_ref[...] = woa_in[...]
        wout_ref[...] = wout_in[...]

    subs = [slice(i * F_SUB, (i + 1) * F_SUB) for i in range(T // F_SUB)]

    zbs = [_dot(h_ref[rs, :], wz_ref[...]) for rs in subs]
    gbs = [_dot(h_ref[rs, :], wgb_ref[...]) for rs in subs]

    for gi, (o_ref, l_ref) in enumerate(((o1_ref, l1_ref), (o2_ref, l2_ref), (o3_ref, l3_ref))):
        d = ATTN_GROUPS[gi][1]
        for r in range(d):
            rows = pl.ds(r, T // d, stride=d)
            l_scr[gi, rows, :] = l_ref[r]
            for hd in range(A_HPG):
                o_scr[gi * A_HPG + hd, rows, :] = (
                    o_ref[r, :, hd * A_HDIM:(hd + 1) * A_HDIM].astype(F32))

    y_as = []
    for rs, zb in zip(subs, zbs):
        l1, l2, l3 = l_scr[0, rs, :], l_scr[1, rs, :], l_scr[2, rs, :]
        mx = jnp.maximum(jnp.maximum(l1, l2), l3)
        e1, e2, e3 = jnp.exp(l1 - mx), jnp.exp(l2 - mx), jnp.exp(l3 - mx)
        inv = 1.0 / (e1 + e2 + e3)
        w1, w2, w3 = e1 * inv, e2 * inv, e3 * inv
        for hd in range(A_HPG):
            cs = slice(hd * A_HDIM, (hd + 1) * A_HDIM)
            hc = slice(hd, hd + 1)
            o_a = (w1[:, hc] * o_scr[hd, rs, :] + w2[:, hc] * o_scr[A_HPG + hd, rs, :] +
                   w3[:, hc] * o_scr[2 * A_HPG + hd, rs, :])
            oz_scr[rs, cs] = (o_a * _silu(zb[:, cs])).astype(BF16)
        y_as.append(_dot(oz_scr[rs, :], woa_ref[...]))

    deltas = []
    for rs, gb, y_a in zip(subs, gbs, y_as):
        merged = ym_ref[rs, :].astype(F32) + _sigmoid(gb) * y_a
        deltas.append(_dot(merged.astype(BF16), wout_ref[...]))

    for rs, delta in zip(subs, deltas):
        xn = x_ref[rs, :] + gate_ref[...] * delta
        out_ref[rs, :] = (xn * lax.rsqrt(jnp.mean(xn * xn, axis=-1, keepdims=True) + EPS) *
                          gfin_ref[...])


def _final_call(x, h_nat, gate_mod, ym, os_, ls_, wz, wgb, woa, wout, gfin):
    B, S, D = x.shape
    T = F_TILE
    assert S % T == 0 and all(T % (d * 2 * SUBLANES) == 0 for _, d in ATTN_GROUPS)
    tok = lambda w: pl.BlockSpec((None, T, w), lambda b, t: (b, t, 0))
    split = lambda a: pl.BlockSpec((None, a.shape[1], T // a.shape[1], a.shape[3]),
                                   lambda b, t: (b, 0, t, 0))
    per_batch = pl.BlockSpec((None, 1, D), lambda b, t: (b, 0, 0))
    wts = (wz, wgb, woa, wout, gfin)
    return pl.pallas_call(
        _final_kernel,
        grid=(B, S // T),
        in_specs=([tok(D), tok(D), per_batch, tok(D)] + [split(a) for a in os_] +
                  [split(a) for a in ls_] + [_resident(a.shape) for a in wts]),
        out_specs=tok(D),
        out_shape=jax.ShapeDtypeStruct((B, S, D), F32),
        scratch_shapes=[
            pltpu.VMEM((T, A_GW), BF16),
            pltpu.VMEM((N_GROUPS * A_HPG, T, A_HDIM), F32),
            pltpu.VMEM((N_GROUPS, T, LANES), F32),
            pltpu.VMEM(wz.shape, BF16),
            pltpu.VMEM(wgb.shape, BF16),
            pltpu.VMEM(woa.shape, BF16),
            pltpu.VMEM(wout.shape, BF16),
        ],
        compiler_params=pltpu.CompilerParams(dimension_semantics=("arbitrary", "arbitrary"),
                                             vmem_limit_bytes=VMEM_LIMIT),
        name="merge_out_norm",
    )(x, h_nat, gate_mod, ym, *os_, *ls_, *wts)


def _rope_tables():
    inv_freq = ROPE_THETA ** (-jnp.arange(ROPE_HALF, dtype=F32) / ROPE_HALF)
    e = np.zeros((4 * ROPE_HALF, 2 * LANES), np.float32)
    for part in range(2):
        for i in range(ROPE_HALF):
            rc, rs = part * 2 * ROPE_HALF + i, part * 2 * ROPE_HALF + ROPE_HALF + i
            e[rc, i] = 1.0
            e[rc, LANES // 2 + i] = 1.0
            e[rs, LANES + i] = -1.0
            e[rs, LANES + LANES // 2 + i] = 1.0
    return inv_freq.reshape(ROPE_HALF, 1), jnp.asarray(e, BF16)


def _gate_select():
    e = np.zeros((_GATE_STACK_ROWS, 2 * LANES), np.float32)
    for q in range(2):
        for part in range(_N_PARTS):
            for hd in range(M_HEADS):
                e[(q * _N_PARTS + part) * SUBLANES + hd, q * LANES + hd] = 1.0
    return jnp.asarray(e, BF16)


def _cumsum_triangle():
    return jnp.asarray(np.triu(np.ones((M_CHUNK, M_CHUNK), np.float32)), BF16)


def _split_perm(d):
    p = np.zeros((M_CHUNK, M_CHUNK), np.float32)
    out = np.arange(M_CHUNK)
    p[out, (out % (M_CHUNK // d)) * d + out // (M_CHUNK // d)] = 1.0
    return jnp.asarray(p, BF16)


def kernel(x, c, positions, w_ada, b_ada, g_norm, w_in, b_igate, b_fgate, w_conv, b_conv,
           w_q_mlstm, w_k_mlstm, g_mlstm_head, skip_mlstm, w_out_mlstm, w_out_attn, w_out, g_final):
    B, S, D = x.shape
    assert D == D_MODEL and w_ada.shape[0] == 1
    W = M_WIDTH
    QKV = N_GROUPS * A_GW
    o_gate = 4 * W
    o_q = o_gate + 2 * M_HEADS
    o_k, o_v = o_q + QKV, o_q + 2 * QKV
    o_zb = o_q + 3 * QKV
    o_ga = o_zb + A_GW
    o_gb = o_ga + D
    w = w_in[0]

    mod = _mod_call(c, w_ada[0], b_ada[0], g_norm[0])
    shift_mod = mod[0].reshape(B, 1, D)
    a_mod = mod[1].reshape(B, 1, D)
    gate_mod = mod[2].reshape(B, 1, D)

    w_i = w[:, o_gate:o_gate + M_HEADS]
    w_f = w[:, o_gate + M_HEADS:o_gate + 2 * M_HEADS]
    pad_r = jnp.zeros((SUBLANES - M_HEADS, D), F32)
    wgr = jnp.concatenate([w_i.T, pad_r, w_f.T, pad_r], axis=0).astype(BF16)
    zr = jnp.zeros((SUBLANES - M_HEADS,), F32)
    bgr = jnp.concatenate([b_igate[0], zr, b_fgate[0], zr]).reshape(2 * SUBLANES, 1)
    ym, h_nat, h_d4, h_d16 = _mlstm_call(
        x, a_mod, shift_mod, w[:, 0:4 * W].astype(BF16), wgr, bgr, _cumsum_triangle(), _gate_select(),
        _split_perm(ATTN_GROUPS[1][1]), _split_perm(ATTN_GROUPS[2][1]),
        w_conv[0], b_conv[0].reshape(1, W), w_q_mlstm[0].astype(BF16),
        (w_k_mlstm[0] * (M_HDIM ** -0.5)).astype(BF16),
        g_mlstm_head[0].reshape(1, W), skip_mlstm[0].reshape(1, W),
        w[:, o_ga:o_ga + D].astype(BF16), w_out_mlstm[0].astype(BF16))

    freq, esel = _rope_tables()
    os_, ls_ = [], []
    for gi, h_split in enumerate((h_nat.reshape(B, 1, S, D), h_d4, h_d16)):
        assert h_split.shape[1] == ATTN_GROUPS[gi][1]
        gs = slice(gi * A_GW, (gi + 1) * A_GW)
        w_qkv = jnp.concatenate([w[:, o_q:o_k][:, gs], w[:, o_k:o_v][:, gs], w[:, o_v:o_zb][:, gs]],
                                axis=1)
        o_g, l_g = _attn_call(h_split, positions, freq, esel, w_qkv)
        os_.append(o_g)
        ls_.append(l_g)

    return _final_call(
        x, h_nat, gate_mod, ym, os_, ls_,
        w[:, o_zb:o_zb + A_GW].astype(BF16), w[:, o_gb:o_gb + D].astype(BF16),
        w_out_attn[0].astype(BF16), w_out[0].astype(BF16), g_final.reshape(1, D))
```

```python
import jax
import jax.numpy as jnp
import numpy as np
from jax import lax
from jax.experimental import pallas as pl
from jax.experimental.pallas import tpu as pltpu

F32 = jnp.float32
BF16 = jnp.bfloat16

D_MODEL = 1024
EPS = 1e-6
M_HEADS = 4
M_HDIM = 256
M_WIDTH = M_HEADS * M_HDIM
CONV_W = 4
M_CHUNK = 256
M_STEP = 512
M_LAG = 8
ATTN_GROUPS = ((128, 1), (512, 4), (2048, 16))
N_GROUPS = len(ATTN_GROUPS)
A_HPG = 4
A_HDIM = 128
A_GW = A_HPG * A_HDIM
A_SPAN = 128
ROPE_DIM = A_HDIM // 4
ROPE_HALF = ROPE_DIM // 2
ROPE_THETA = 500000.0
A_TILE = 1024
A_LAG = 2
F_TILE = 512
F_SUB = 256
LANES = 128
SUBLANES = 8
VMEM_LIMIT = 56 * 1024 * 1024

assert all(w // d == A_SPAN for w, d in ATTN_GROUPS)
_ROPE_PIECES = ((0, ROPE_HALF), (ROPE_DIM, LANES // 2 + ROPE_HALF), (ROPE_HALF, ROPE_DIM),
                (LANES // 2 + ROPE_HALF, A_HDIM))


def _sigmoid(x):
    return 1.0 / (1.0 + jnp.exp(-x))


def _silu(x):
    return x * _sigmoid(x)


def _log_sigmoid(x):
    return jnp.minimum(x, 0.0) - jnp.log(1.0 + jnp.exp(-jnp.abs(x)))


def _mod_norm(x, a_row, shift_row):
    ms = jnp.mean(x * x, axis=-1, keepdims=True)
    return x * lax.rsqrt(ms + EPS) * a_row + shift_row


def _dot(a, b):
    return jnp.dot(a, b, preferred_element_type=F32)


def _dot_nt(a, b):
    return lax.dot_general(a, b, (((1,), (1,)), ((), ())), preferred_element_type=F32)


def _dot_tn(a, b):
    return lax.dot_general(a, b, (((0,), (0,)), ((), ())), preferred_element_type=F32)


def _run_staggered(stage_generators, lag):
    pending, live, rnd = list(stage_generators), [], 0
    while pending or live:
        if pending and rnd % lag == 0:
            live.append(pending.pop(0))
        for g in list(live):
            try:
                next(g)
            except StopIteration:
                live.remove(g)
        rnd += 1


def _resident(shape):
    zeros = (0,) * len(shape)
    return pl.BlockSpec(shape, lambda *_: zeros, pipeline_mode=pl.Buffered(1))


def _mod_kernel(c_ref, w_ref, b_ref, g_ref, o_ref):
    j = pl.program_id(0)
    val = _dot(_silu(c_ref[...]), w_ref[...]) + b_ref[...]
    o_ref[...] = jnp.where(j == 1, g_ref[...] * (1.0 + val), val)


def _mod_call(c, w_ada, b_ada, g_norm):
    B = c.shape[0]
    return pl.pallas_call(
        _mod_kernel,
        grid=(3,),
        in_specs=[
            pl.BlockSpec((B, D_MODEL), lambda j: (0, 0)),
            pl.BlockSpec((D_MODEL, D_MODEL), lambda j: (0, j)),
            pl.BlockSpec((1, D_MODEL), lambda j: (0, j)),
            pl.BlockSpec((1, D_MODEL), lambda j: (0, 0)),
        ],
        out_specs=pl.BlockSpec((None, B, D_MODEL), lambda j: (j, 0, 0)),
        out_shape=jax.ShapeDtypeStruct((3, B, D_MODEL), F32),
        compiler_params=pltpu.CompilerParams(dimension_semantics=("arbitrary",),
                                             vmem_limit_bytes=VMEM_LIMIT),
        name="adaln_mod",
    )(c, w_ada, b_ada.reshape(1, 3 * D_MODEL), g_norm.reshape(1, D_MODEL))


def _scan_rows(x, op, identity):
    n = x.shape[0]
    row = lax.broadcasted_iota(jnp.int32, x.shape, 0)
    sh = 1
    while sh < n:
        x = op(x, jnp.where(row >= sh, pltpu.roll(x, sh, 0), identity))
        sh *= 2
    return x


def _bf16_parts(x):
    hi = x.astype(BF16).astype(F32)
    mid = (x - hi).astype(BF16).astype(F32)
    return [hi, mid, x - hi - mid]


_N_PARTS = 3
_CUMSUM_ROWS = 4 * SUBLANES
_GATE_STACK_ROWS = 8 * SUBLANES


def _mlstm_kernel(x_ref, a_ref, shift_ref, wm_in, wgr_ref, bgr_ref, tri_ref, gsel_ref, p4_ref,
                  p16_ref, wconv_ref, bconv_ref, wq_ref, wk_ref, ghead_ref, skip_ref, wga_in,
                  wout_in, o_ref, hnat_ref, hd4_ref, hd16_ref, c_scr, n_scr, m_scr, xa_scr, hm_scr,
                  wm_ref, wga_ref, wout_ref):
    L = M_CHUNK
    W = M_WIDTH
    TAIL = SUBLANES

    @pl.when(pl.program_id(1) == 0)
    def _():
        c_scr[...] = jnp.zeros_like(c_scr)
        n_scr[...] = jnp.zeros_like(n_scr)
        m_scr[...] = jnp.zeros_like(m_scr)
        xa_scr[0:TAIL, :] = jnp.zeros((TAIL, W), F32)

    @pl.when((pl.program_id(0) == 0) & (pl.program_id(1) == 0))
    def _():
        wm_ref[...] = wm_in[...]
        wga_ref[...] = wga_in[...]
        wout_ref[...] = wout_in[...]

    causal = (lax.broadcasted_iota(jnp.int32, (L, L), 0) >=
              lax.broadcasted_iota(jnp.int32, (L, L), 1))

    def chunk_stages(ci):
        rows = slice(ci * L, (ci + 1) * L)
        hnat_ref[rows, :] = _mod_norm(x_ref[rows, :], a_ref[...], shift_ref[...]).astype(BF16)
        hb = hnat_ref[rows, :]

        def split_to(p_ref, ref):
            d = ref.shape[0]
            perm = _dot(p_ref[...], hb).astype(BF16)
            for r in range(d):
                ref[r, ci * (L // d):(ci + 1) * (L // d), :] = perm[r * (L // d):(r + 1) * (L // d), :]

        xa_scr[TAIL:TAIL + L, :] = _dot(hb, wm_ref[:, 0:W])
        gr = _dot_nt(wgr_ref[...], hb) + bgr_ref[...]
        yield
        split_to(p4_ref, hd4_ref)
        lf = _bf16_parts(_log_sigmoid(gr[SUBLANES:2 * SUBLANES, :]))
        lf.append(jnp.zeros((_CUMSUM_ROWS - _N_PARTS * SUBLANES, L), F32))
        csum = _dot(jnp.concatenate(lf, axis=0).astype(BF16), tri_ref[...])
        yield
        split_to(p16_ref, hd16_ref)
        a_row = csum[0:SUBLANES] + csum[SUBLANES:2 * SUBLANES] + csum[2 * SUBLANES:3 * SUBLANES]
        r_row = gr[0:SUBLANES, :] - a_row
        stack = _bf16_parts(r_row) + _bf16_parts(a_row)
        stack.append(jnp.zeros((_GATE_STACK_ROWS - 2 * _N_PARTS * SUBLANES, L), F32))
        rc_ac = _dot_tn(jnp.concatenate(stack, axis=0).astype(BF16), gsel_ref[...])
        ga = _dot(hb, wga_ref[...])
        yield

        xcs, vbs = [], []
        for hd in range(M_HEADS):
            cs = slice(hd * M_HDIM, (hd + 1) * M_HDIM)
            conv = bconv_ref[:, cs]
            for j in range(CONV_W):
                off = TAIL - (CONV_W - 1) + j
                conv = conv + wconv_ref[j:j + 1, cs] * xa_scr[off:off + L, cs]
            xcs.append(_silu(conv))
            vbs.append(_dot(hb, wm_ref[:, W + hd * M_HDIM:W + (hd + 1) * M_HDIM]).astype(BF16))
            if hd == M_HEADS - 1:
                xa_scr[0:TAIL, :] = xa_scr[L:L + TAIL, :]
            yield

        r_col, a_col = rc_ac[:, 0:LANES], rc_ac[:, LANES:2 * LANES]
        m_prev = m_scr[0:1, :]
        big_m = jnp.maximum(_scan_rows(r_col, jnp.maximum, -jnp.inf), m_prev)
        m_last = big_m[L - 1:L, :]
        w_inter_all = jnp.exp(m_prev - big_m)
        floor_all = jnp.exp(-a_col - big_m)
        kscale_all = jnp.exp(r_col - m_last)
        decay_all = jnp.exp(m_prev - m_last)
        m_scr[0:1, :] = a_col[L - 1:L, :] + m_last
        yield

        for hd in range(M_HEADS):
            cs = slice(hd * M_HDIM, (hd + 1) * M_HDIM)
            hc = slice(hd, hd + 1)
            xc_h, vb = xcs[hd], vbs[hd]
            xcb = xc_h.astype(BF16)
            qf = _dot(xcb, wq_ref[hd])
            qb = qf.astype(BF16)
            kf = _dot(xcb, wk_ref[hd])
            kb = kf.astype(BF16)

            w_intra = jnp.exp(jnp.where(causal, r_row[hc, :] - big_m[:, hc], -jnp.inf))
            s = _dot_nt(qb, kb) * w_intra
            w_inter = w_inter_all[:, hc]
            c_old = c_scr[hd]
            n_old = n_scr[hc, :]
            num = _dot(s.astype(BF16), vb) + w_inter * _dot(qb, c_old.astype(BF16))
            den = (jnp.sum(s, axis=1, keepdims=True) +
                   w_inter * jnp.sum(qf * n_old, axis=1, keepdims=True))
            hout = num / jnp.maximum(jnp.abs(den), floor_all[:, hc])

            kw = kscale_all[:, hc] * kf
            decay = decay_all[:, hc]
            c_scr[hd] = decay * c_old + _dot_tn(kw.astype(BF16), vb)
            n_scr[hc, :] = decay * n_old + jnp.sum(kw, axis=0, keepdims=True)

            oa = _dot(hb, wm_ref[:, 2 * W + hd * M_HDIM:2 * W + (hd + 1) * M_HDIM])
            hm = _sigmoid(oa) * hout
            hm = hm * lax.rsqrt(jnp.mean(hm * hm, axis=-1, keepdims=True) + EPS) * ghead_ref[:, cs]
            hm = hm + skip_ref[:, cs] * xc_h
            za = _dot(hb, wm_ref[:, 3 * W + hd * M_HDIM:3 * W + (hd + 1) * M_HDIM])
            hm_scr[rows, cs] = (hm * _silu(za)).astype(BF16)
            yield

        y_m = _dot(hm_scr[rows, :], wout_ref[...])
        o_ref[rows, :] = (_sigmoid(ga) * y_m).astype(o_ref.dtype)

    _run_staggered([chunk_stages(ci) for ci in range(M_STEP // L)], M_LAG)


def _mlstm_call(x, a_mod, shift_mod, wm, wgr, bgr, tri, gsel, p4, p16, wconv, bconv, wq, wk, ghead, skip,
                wga, wout):
    B, S, D = x.shape
    L, T = M_CHUNK, M_STEP
    d4, d16 = ATTN_GROUPS[1][1], ATTN_GROUPS[2][1]
    assert S % T == 0 and T % L == 0 and L % (d16 * 2 * SUBLANES) == 0
    tok = pl.BlockSpec((None, T, D), lambda b, c: (b, c, 0))
    per_batch = pl.BlockSpec((None, 1, D), lambda b, c: (b, 0, 0))
    split = lambda d: pl.BlockSpec((None, d, T // d, D), lambda b, c: (b, 0, c, 0))
    args = (wm, wgr, bgr, tri, gsel, p4, p16, wconv, bconv, wq, wk, ghead, skip, wga, wout)
    return pl.pallas_call(
        _mlstm_kernel,
        grid=(B, S // T),
        in_specs=[tok, per_batch, per_batch] + [_resident(a.shape) for a in args],
        out_specs=[tok, tok, split(d4), split(d16)],
        out_shape=[
            jax.ShapeDtypeStruct((B, S, D), BF16),
            jax.ShapeDtypeStruct((B, S, D), BF16),
            jax.ShapeDtypeStruct((B, d4, S // d4, D), BF16),
            jax.ShapeDtypeStruct((B, d16, S // d16, D), BF16),
        ],
        scratch_shapes=[
            pltpu.VMEM((M_HEADS, M_HDIM, M_HDIM), F32),
            pltpu.VMEM((SUBLANES, M_HDIM), F32),
            pltpu.VMEM((SUBLANES, LANES), F32),
            pltpu.VMEM((L + SUBLANES, M_WIDTH), F32),
            pltpu.VMEM((T, M_WIDTH), BF16),
            pltpu.VMEM(wm.shape, BF16),
            pltpu.VMEM(wga.shape, BF16),
            pltpu.VMEM(wout.shape, BF16),
        ],
        compiler_params=pltpu.CompilerParams(dimension_semantics=("arbitrary", "arbitrary"),
                                             vmem_limit_bytes=VMEM_LIMIT),
        name="mlstm_branch",
    )(x, a_mod, shift_mod, *args)


def _attn_kernel(h_ref, pos_ref, freq_ref, esel_ref, w_in, o_ref, lse_ref, k_scr, v_scr, w_ref):
    n_res, n = h_ref.shape[0], h_ref.shape[1]
    ub = pl.program_id(2)

    @pl.when((pl.program_id(0) == 0) & (pl.program_id(1) == 0) & (ub == 0))
    def _():
        for blk in range(2 * A_HPG):
            cs_ = slice(blk * A_HDIM, (blk + 1) * A_HDIM)
            src = w_in[:, cs_]
            w_ref[:, cs_] = jnp.concatenate([src[:, lo:hi] for lo, hi in _ROPE_PIECES],
                                            axis=1).astype(BF16)
        w_ref[:, 2 * A_GW:3 * A_GW] = w_in[:, 2 * A_GW:3 * A_GW].astype(BF16)

    @pl.when(ub == 0)
    def _():
        k_scr[:, 0:A_SPAN, :] = jnp.zeros((n_res, A_SPAN, A_GW), BF16)
        v_scr[:, 0:A_SPAN, :] = jnp.zeros((n_res, A_SPAN, A_GW), BF16)

    @pl.when(ub > 0)
    def _():
        k_scr[:, 0:A_SPAN, :] = k_scr[:, n:n + A_SPAN, :]
        v_scr[:, 0:A_SPAN, :] = v_scr[:, n:n + A_SPAN, :]

    lane = lax.broadcasted_iota(jnp.int32, (1, LANES), 1)
    rot_lane = (lane % (LANES // 2)) < ROPE_HALF
    p_idx = lax.broadcasted_iota(jnp.int32, (A_SPAN, 2 * A_SPAN), 0)
    c_idx = lax.broadcasted_iota(jnp.int32, (A_SPAN, 2 * A_SPAN), 1)
    band = (c_idx >= p_idx) & (c_idx <= p_idx + A_SPAN)
    lane_o = lax.broadcasted_iota(jnp.int32, (A_SPAN, LANES), 1)
    scale = A_HDIM ** -0.5

    def residue_stages(rr):
        hb = h_ref[rr]

        ang = freq_ref[...] * pos_ref[rr].astype(F32)
        cs = jnp.concatenate([jnp.cos(ang), jnp.sin(ang)], axis=0)
        cs_hi = cs.astype(BF16)
        cs_lo = (cs - cs_hi.astype(F32)).astype(BF16)
        tab = _dot_tn(jnp.concatenate([cs_hi, cs_lo], axis=0), esel_ref[...])
        q = _dot(hb, w_ref[:, 0:A_GW])
        k = _dot(hb, w_ref[:, A_GW:2 * A_GW])
        yield

        cos_t = tab[:, 0:LANES] + jnp.where(rot_lane, 0.0, 1.0)
        sin_t = tab[:, LANES:2 * LANES]

        def rope(t):
            return t * cos_t + pltpu.roll(t, LANES // 2, 1) * sin_t

        qs = []
        for hd in range(A_HPG):
            cs_ = slice(hd * A_HDIM, (hd + 1) * A_HDIM)
            qs.append((rope(q[:, cs_]) * scale).astype(BF16))
            k_scr[rr, A_SPAN:A_SPAN + n, cs_] = rope(k[:, cs_]).astype(BF16)
        v_scr[rr, A_SPAN:A_SPAN + n, :] = _dot(hb, w_ref[:, 2 * A_GW:3 * A_GW]).astype(BF16)
        yield

        for jb in range(n // A_SPAN):
            rows = slice(jb * A_SPAN, (jb + 1) * A_SPAN)
            krows = slice(jb * A_SPAN, (jb + 2) * A_SPAN)
            if jb == 0:
                valid = band & (c_idx >= jnp.where(ub == 0, A_SPAN, 0))
            else:
                valid = band
            lse_blk = jnp.zeros((A_SPAN, LANES), F32)
            for hd in range(A_HPG):
                cs_ = slice(hd * A_HDIM, (hd + 1) * A_HDIM)
                s = _dot_nt(qs[hd][rows, :], k_scr[rr, krows, cs_])
                s = jnp.where(valid, s, -jnp.inf)
                mx = jnp.max(s, axis=1, keepdims=True)
                e = jnp.exp(s - mx)
                den = jnp.sum(e, axis=1, keepdims=True)
                o = _dot(e.astype(BF16), v_scr[rr, krows, cs_]) / den
                o_ref[rr, rows, cs_] = o.astype(o_ref.dtype)
                lse_blk = jnp.where(lane_o == hd, mx + jnp.log(den), lse_blk)
                if hd % 2 == 1:
                    yield
            lse_ref[rr, rows, :] = lse_blk

    _run_staggered([residue_stages(rr) for rr in range(n_res)], A_LAG)


def _attn_call(h_split, positions, freq, esel, w_qkv):
    B, d, U, D = h_split.shape
    n = min(A_TILE, U)
    n_res = min(d, A_TILE // n)
    assert U % n == 0 and n % A_SPAN == 0 and d % n_res == 0
    pos = positions.reshape(B, U, d).transpose(0, 2, 1).reshape(B, d, 1, U)
    tok = lambda w: pl.BlockSpec((None, n_res, n, w), lambda b, r, u: (b, r, u, 0))
    return pl.pallas_call(
        _attn_kernel,
        grid=(B, d // n_res, U // n),
        in_specs=[
            tok(D),
            pl.BlockSpec((None, n_res, 1, n), lambda b, r, u: (b, r, 0, u)),
            _resident(freq.shape), _resident(esel.shape), _resident(w_qkv.shape),
        ],
        out_specs=[tok(A_GW), tok(LANES)],
        out_shape=[
            jax.ShapeDtypeStruct((B, d, U, A_GW), BF16),
            jax.ShapeDtypeStruct((B, d, U, LANES), F32),
        ],
        scratch_shapes=[
            pltpu.VMEM((n_res, A_SPAN + n, A_GW), BF16),
            pltpu.VMEM((n_res, A_SPAN + n, A_GW), BF16),
            pltpu.VMEM(w_qkv.shape, BF16),
        ],
        compiler_params=pltpu.CompilerParams(
            dimension_semantics=("arbitrary", "arbitrary", "arbitrary"),
            vmem_limit_bytes=VMEM_LIMIT),
        name=f"dilated_attn_d{d}",
    )(h_split, pos, freq, esel, w_qkv)


def _final_kernel(x_ref, h_ref, gate_ref, ym_ref, o1_ref, o2_ref, o3_ref, l1_ref, l2_ref, l3_ref,
                  wz_in, wgb_in, woa_in, wout_in, gfin_ref, out_ref, oz_scr, o_scr, l_scr,
                  wz_ref, wgb_ref, woa_ref, wout_ref):
    T = F_TILE

    @pl.when((pl.program_id(0) == 0) & (pl.program_id(1) == 0))
    def _():
        wz_ref[...] = wz_in[...]
        wgb_ref[...] = wgb_in[...]
        woa_ref[...] = woa_in[...]
        wout_ref[...] = wout_in[...]

    subs = [slice(i * F_SUB, (i + 1) * F_SUB) for i in range(T // F_SUB)]

    zbs = [_dot(h_ref[rs, :], wz_ref[...]) for rs in subs]
    gbs = [_dot(h_ref[rs, :], wgb_ref[...]) for rs in subs]

    for gi, (o_ref, l_ref) in enumerate(((o1_ref, l1_ref), (o2_ref, l2_ref), (o3_ref, l3_ref))):
        d = ATTN_GROUPS[gi][1]
        for r in range(d):
            rows = pl.ds(r, T // d, stride=d)
            l_scr[gi, rows, :] = l_ref[r]
            for hd in range(A_HPG):
                o_scr[gi * A_HPG + hd, rows, :] = (
                    o_ref[r, :, hd * A_HDIM:(hd + 1) * A_HDIM].astype(F32))

    y_as = []
    for rs, zb in zip(subs, zbs):
        l1, l2, l3 = l_scr[0, rs, :], l_scr[1, rs, :], l_scr[2, rs, :]
        mx = jnp.maximum(jnp.maximum(l1, l2), l3)
        e1, e2, e3 = jnp.exp(l1 - mx), jnp.exp(l2 - mx), jnp.exp(l3 - mx)
        inv = 1.0 / (e1 + e2 + e3)
        w1, w2, w3 = e1 * inv, e2 * inv, e3 * inv
        for hd in range(A_HPG):
            cs = slice(hd * A_HDIM, (hd + 1) * A_HDIM)
            hc = slice(hd, hd + 1)
            o_a = (w1[:, hc] * o_scr[hd, rs, :] + w2[:, hc] * o_scr[A_HPG + hd, rs, :] +
                   w3[:, hc] * o_scr[2 * A_HPG + hd, rs, :])
            oz_scr[rs, cs] = (o_a * _silu(zb[:, cs])).astype(BF16)
        y_as.append(_dot(oz_scr[rs, :], woa_ref[...]))

    deltas = []
    for rs, gb, y_a in zip(subs, gbs, y_as):
        merged = ym_ref[rs, :].astype(F32) + _sigmoid(gb) * y_a
        deltas.append(_dot(merged.astype(BF16), wout_ref[...]))

    for rs, delta in zip(subs, deltas):
        xn = x_ref[rs, :] + gate_ref[...] * delta
        out_ref[rs, :] = (xn * lax.rsqrt(jnp.mean(xn * xn, axis=-1, keepdims=True) + EPS) *
                          gfin_ref[...])


def _final_call(x, h_nat, gate_mod, ym, os_, ls_, wz, wgb, woa, wout, gfin):
    B, S, D = x.shape
    T = F_TILE
    assert S % T == 0 and all(T % (d * 2 * SUBLANES) == 0 for _, d in ATTN_GROUPS)
    tok = lambda w: pl.BlockSpec((None, T, w), lambda b, t: (b, t, 0))
    split = lambda a: pl.BlockSpec((None, a.shape[1], T // a.shape[1], a.shape[3]),
                                   lambda b, t: (b, 0, t, 0))
    per_batch = pl.BlockSpec((None, 1, D), lambda b, t: (b, 0, 0))
    wts = (wz, wgb, woa, wout, gfin)
    return pl.pallas_call(
        _final_kernel,
        grid=(B, S // T),
        in_specs=([tok(D), tok(D), per_batch, tok(D)] + [split(a) for a in os_] +
                  [split(a) for a in ls_] + [_resident(a.shape) for a in wts]),
        out_specs=tok(D),
        out_shape=jax.ShapeDtypeStruct((B, S, D), F32),
        scratch_shapes=[
            pltpu.VMEM((T, A_GW), BF16),
            pltpu.VMEM((N_GROUPS * A_HPG, T, A_HDIM), F32),
            pltpu.VMEM((N_GROUPS, T, LANES), F32),
            pltpu.VMEM(wz.shape, BF16),
            pltpu.VMEM(wgb.shape, BF16),
            pltpu.VMEM(woa.shape, BF16),
            pltpu.VMEM(wout.shape, BF16),
        ],
        compiler_params=pltpu.CompilerParams(dimension_semantics=("arbitrary", "arbitrary"),
                                             vmem_limit_bytes=VMEM_LIMIT),
        name="merge_out_norm",
    )(x, h_nat, gate_mod, ym, *os_, *ls_, *wts)


def _rope_tables():
    inv_freq = ROPE_THETA ** (-jnp.arange(ROPE_HALF, dtype=F32) / ROPE_HALF)
    e = np.zeros((4 * ROPE_HALF, 2 * LANES), np.float32)
    for part in range(2):
        for i in range(ROPE_HALF):
            rc, rs = part * 2 * ROPE_HALF + i, part * 2 * ROPE_HALF + ROPE_HALF + i
            e[rc, i] = 1.0
            e[rc, LANES // 2 + i] = 1.0
            e[rs, LANES + i] = -1.0
            e[rs, LANES + LANES // 2 + i] = 1.0
    return inv_freq.reshape(ROPE_HALF, 1), jnp.asarray(e, BF16)


def _gate_select():
    e = np.zeros((_GATE_STACK_ROWS, 2 * LANES), np.float32)
    for q in range(2):
        for part in range(_N_PARTS):
            for hd in range(M_HEADS):
                e[(q * _N_PARTS + part) * SUBLANES + hd, q * LANES + hd] = 1.0
    return jnp.asarray(e, BF16)


def _cumsum_triangle():
    return jnp.asarray(np.triu(np.ones((M_CHUNK, M_CHUNK), np.float32)), BF16)


def _split_perm(d):
    p = np.zeros((M_CHUNK, M_CHUNK), np.float32)
    out = np.arange(M_CHUNK)
    p[out, (out % (M_CHUNK // d)) * d + out // (M_CHUNK // d)] = 1.0
    return jnp.asarray(p, BF16)


def kernel(x, c, positions, w_ada, b_ada, g_norm, w_in, b_igate, b_fgate, w_conv, b_conv,
           w_q_mlstm, w_k_mlstm, g_mlstm_head, skip_mlstm, w_out_mlstm, w_out_attn, w_out, g_final):
    B, S, D = x.shape
    assert D == D_MODEL and w_ada.shape[0] == 1
    W = M_WIDTH
    QKV = N_GROUPS * A_GW
    o_gate = 4 * W
    o_q = o_gate + 2 * M_HEADS
    o_k, o_v = o_q + QKV, o_q + 2 * QKV
    o_zb = o_q + 3 * QKV
    o_ga = o_zb + A_GW
    o_gb = o_ga + D
    w = w_in[0]

    mod = _mod_call(c, w_ada[0], b_ada[0], g_norm[0])
    shift_mod = mod[0].reshape(B, 1, D)
    a_mod = mod[1].reshape(B, 1, D)
    gate_mod = mod[2].reshape(B, 1, D)

    w_i = w[:, o_gate:o_gate + M_HEADS]
    w_f = w[:, o_gate + M_HEADS:o_gate + 2 * M_HEADS]
    pad_r = jnp.zeros((SUBLANES - M_HEADS, D), F32)
    wgr = jnp.concatenate([w_i.T, pad_r, w_f.T, pad_r], axis=0).astype(BF16)
    zr = jnp.zeros((SUBLANES - M_HEADS,), F32)
    bgr = jnp.concatenate([b_igate[0], zr, b_fgate[0], zr]).reshape(2 * SUBLANES, 1)
    ym, h_nat, h_d4, h_d16 = _mlstm_call(
        x, a_mod, shift_mod, w[:, 0:4 * W].astype(BF16), wgr, bgr, _cumsum_triangle(), _gate_select(),
        _split_perm(ATTN_GROUPS[1][1]), _split_perm(ATTN_GROUPS[2][1]),
        w_conv[0], b_conv[0].reshape(1, W), w_q_mlstm[0].astype(BF16),
        (w_k_mlstm[0] * (M_HDIM ** -0.5)).astype(BF16),
        g_mlstm_head[0].reshape(1, W), skip_mlstm[0].reshape(1, W),
        w[:, o_ga:o_ga + D].astype(BF16), w_out_mlstm[0].astype(BF16))

    freq, esel = _rope_tables()
    os_, ls_ = [], []
    for gi, h_split in enumerate((h_nat.reshape(B, 1, S, D), h_d4, h_d16)):
        assert h_split.shape[1] == ATTN_GROUPS[gi][1]
        gs = slice(gi * A_GW, (gi + 1) * A_GW)
        w_qkv = jnp.concatenate([w[:, o_q:o_k][:, gs], w[:, o_k:o_v][:, gs], w[:, o_v:o_zb][:, gs]],
                                axis=1)
        o_g, l_g = _attn_call(h_split, positions, freq, esel, w_qkv)
        os_.append(o_g)
        ls_.append(l_g)

    return _final_call(
        x, h_nat, gate_mod, ym, os_, ls_,
        w[:, o_zb:o_zb + A_GW].astype(BF16), w[:, o_gb:o_gb + D].astype(BF16),
        w_out_attn[0].astype(BF16), w_out[0].astype(BF16), g_final.reshape(1, D))
```

```python
import jax
import jax.numpy as jnp
import numpy as np
from jax import lax
from jax.experimental import pallas as pl
from jax.experimental.pallas import tpu as pltpu

F32 = jnp.float32
BF16 = jnp.bfloat16

D_MODEL = 1024
EPS = 1e-6
M_HEADS = 4
M_HDIM = 256
M_WIDTH = M_HEADS * M_HDIM
CONV_W = 4
M_CHUNK = 256
M_BPS = 2
M_LAG = 5
ATTN_GROUPS = ((128, 1), (512, 4), (2048, 16))
N_GROUPS = len(ATTN_GROUPS)
A_HPG = 4
A_HDIM = 128
A_GW = A_HPG * A_HDIM
A_SPAN = 128
ROPE_DIM = A_HDIM // 4
ROPE_HALF = ROPE_DIM // 2
ROPE_THETA = 500000.0
A_TILE = 1024
A_LAG = 2
F_TILE = 512
F_SUB = 256
LANES = 128
SUBLANES = 8
VMEM_LIMIT = 56 * 1024 * 1024

assert all(w // d == A_SPAN for w, d in ATTN_GROUPS)
_ROPE_PIECES = ((0, ROPE_HALF), (ROPE_DIM, LANES // 2 + ROPE_HALF), (ROPE_HALF, ROPE_DIM),
                (LANES // 2 + ROPE_HALF, A_HDIM))


def _sigmoid(x):
    return 1.0 / (1.0 + jnp.exp(-x))


def _silu(x):
    return x * _sigmoid(x)


def _log_sigmoid(x):
    return jnp.minimum(x, 0.0) - jnp.log(1.0 + jnp.exp(-jnp.abs(x)))


def _mod_norm(x, a_row, shift_row):
    ms = jnp.mean(x * x, axis=-1, keepdims=True)
    return x * lax.rsqrt(ms + EPS) * a_row + shift_row


def _dot(a, b):
    return jnp.dot(a, b, preferred_element_type=F32)


def _dot_nt(a, b):
    return lax.dot_general(a, b, (((1,), (1,)), ((), ())), preferred_element_type=F32)


def _dot_tn(a, b):
    return lax.dot_general(a, b, (((0,), (0,)), ((), ())), preferred_element_type=F32)


def _run_staggered(stage_generators, lag):
    pending, live, rnd = list(stage_generators), [], 0
    while pending or live:
        if pending and rnd % lag == 0:
            live.append(pending.pop(0))
        for g in list(live):
            try:
                next(g)
            except StopIteration:
                live.remove(g)
        rnd += 1


def _resident(shape):
    zeros = (0,) * len(shape)
    return pl.BlockSpec(shape, lambda *_: zeros, pipeline_mode=pl.Buffered(1))


def _mod_kernel(c_ref, w_ref, b_ref, g_ref, o_ref):
    j = pl.program_id(0)
    val = _dot(_silu(c_ref[...]), w_ref[...]) + b_ref[...]
    o_ref[...] = jnp.where(j == 1, g_ref[...] * (1.0 + val), val)


def _mod_call(c, w_ada, b_ada, g_norm):
    B = c.shape[0]
    return pl.pallas_call(
        _mod_kernel,
        grid=(3,),
        in_specs=[
            pl.BlockSpec((B, D_MODEL), lambda j: (0, 0)),
            pl.BlockSpec((D_MODEL, D_MODEL), lambda j: (0, j)),
            pl.BlockSpec((1, D_MODEL), lambda j: (0, j)),
            pl.BlockSpec((1, D_MODEL), lambda j: (0, 0)),
        ],
        out_specs=pl.BlockSpec((None, B, D_MODEL), lambda j: (j, 0, 0)),
        out_shape=jax.ShapeDtypeStruct((3, B, D_MODEL), F32),
        compiler_params=pltpu.CompilerParams(dimension_semantics=("arbitrary",),
                                             vmem_limit_bytes=VMEM_LIMIT),
        name="adaln_mod",
    )(c, w_ada, b_ada.reshape(1, 3 * D_MODEL), g_norm.reshape(1, D_MODEL))


def _scan_rows(x, op, identity):
    n = x.shape[0]
    row = lax.broadcasted_iota(jnp.int32, x.shape, 0)
    sh = 1
    while sh < n:
        x = op(x, jnp.where(row >= sh, pltpu.roll(x, sh, 0), identity))
        sh *= 2
    return x


def _bf16_parts(x):
    hi = x.astype(BF16).astype(F32)
    mid = (x - hi).astype(BF16).astype(F32)
    return [hi, mid, x - hi - mid]


_N_PARTS = 3
_CUMSUM_ROWS = 4 * SUBLANES
_GATE_STACK_ROWS = 8 * SUBLANES


def _mlstm_kernel(x_ref, a_ref, shift_ref, wm_in, wgr_ref, bgr_ref, tri_ref, gsel_ref, p4_ref,
                  p16_ref, wconv_ref, bconv_ref, wq_ref, wk_ref, ghead_ref, skip_ref, wga_in,
                  wout_in, o_ref, hnat_ref, hd4_ref, hd16_ref, c_scr, n_scr, m_scr, xa_scr, hm_scr,
                  wm_ref, wga_ref, wout_ref):
    L = M_CHUNK
    W = M_WIDTH
    TAIL = SUBLANES

    @pl.when(pl.program_id(1) == 0)
    def _():
        c_scr[...] = jnp.zeros_like(c_scr)
        n_scr[...] = jnp.zeros_like(n_scr)
        m_scr[...] = jnp.zeros_like(m_scr)
        xa_scr[:, 0:TAIL, :] = jnp.zeros((M_BPS, TAIL, W), F32)

    @pl.when((pl.program_id(0) == 0) & (pl.program_id(1) == 0))
    def _():
        wm_ref[...] = wm_in[...]
        wga_ref[...] = wga_in[...]
        wout_ref[...] = wout_in[...]

    causal = (lax.broadcasted_iota(jnp.int32, (L, L), 0) >=
              lax.broadcasted_iota(jnp.int32, (L, L), 1))

    for bi in range(M_BPS):
        hnat_ref[bi] = _mod_norm(x_ref[bi], a_ref[bi], shift_ref[bi]).astype(BF16)
    hb_all = hnat_ref[...].reshape(M_BPS * L, D_MODEL)
    shared = {}

    def proj(name, w_cols):
        if name not in shared:
            shared[name] = _dot(hb_all, w_cols())
        return shared[name]

    def batch_stages(bi):
        rows = slice(bi * L, (bi + 1) * L)
        hb = hnat_ref[bi]

        def split_to(p_ref, ref):
            d = ref.shape[1]
            perm = _dot(p_ref[...], hb).astype(BF16)
            for r in range(d):
                ref[bi, r] = perm[r * (L // d):(r + 1) * (L // d), :]

        xa_scr[bi, TAIL:TAIL + L, :] = proj("xa", lambda: wm_ref[:, 0:W])[rows, :]
        gr = _dot_nt(wgr_ref[...], hb) + bgr_ref[...]
        yield
        split_to(p4_ref, hd4_ref)
        lf = _bf16_parts(_log_sigmoid(gr[SUBLANES:2 * SUBLANES, :]))
        lf.append(jnp.zeros((_CUMSUM_ROWS - _N_PARTS * SUBLANES, L), F32))
        csum = _dot(jnp.concatenate(lf, axis=0).astype(BF16), tri_ref[...])
        yield
        split_to(p16_ref, hd16_ref)
        a_row = csum[0:SUBLANES] + csum[SUBLANES:2 * SUBLANES] + csum[2 * SUBLANES:3 * SUBLANES]
        r_row = gr[0:SUBLANES, :] - a_row
        stack = _bf16_parts(r_row) + _bf16_parts(a_row)
        stack.append(jnp.zeros((_GATE_STACK_ROWS - 2 * _N_PARTS * SUBLANES, L), F32))
        rc_ac = _dot_tn(jnp.concatenate(stack, axis=0).astype(BF16), gsel_ref[...])
        ga = proj("ga", lambda: wga_ref[...])[rows, :]
        yield

        xcs, vbs = [], []
        for hd in range(M_HEADS):
            cs = slice(hd * M_HDIM, (hd + 1) * M_HDIM)
            conv = bconv_ref[:, cs]
            for j in range(CONV_W):
                off = TAIL - (CONV_W - 1) + j
                conv = conv + wconv_ref[j:j + 1, cs] * xa_scr[bi, off:off + L, cs]
            xcs.append(_silu(conv))
            v_all = proj(("v", hd), lambda: wm_ref[:, W + hd * M_HDIM:W + (hd + 1) * M_HDIM])
            vbs.append(v_all[rows, :].astype(BF16))
            if hd == M_HEADS - 1:
                xa_scr[bi, 0:TAIL, :] = xa_scr[bi, L:L + TAIL, :]
            yield

        r_col, a_col = rc_ac[:, 0:LANES], rc_ac[:, LANES:2 * LANES]
        m_prev = m_scr[bi, 0:1, :]
        big_m = jnp.maximum(_scan_rows(r_col, jnp.maximum, -jnp.inf), m_prev)
        m_last = big_m[L - 1:L, :]
        w_inter_all = jnp.exp(m_prev - big_m)
        floor_all = jnp.exp(-a_col - big_m)
        kscale_all = jnp.exp(r_col - m_last)
        decay_all = jnp.exp(m_prev - m_last)
        m_scr[bi, 0:1, :] = a_col[L - 1:L, :] + m_last
        yield

        for hd in range(M_HEADS):
            cs = slice(hd * M_HDIM, (hd + 1) * M_HDIM)
            hc = slice(hd, hd + 1)
            xc_h, vb = xcs[hd], vbs[hd]
            xcb = xc_h.astype(BF16)
            qf = _dot(xcb, wq_ref[hd])
            qb = qf.astype(BF16)
            kf = _dot(xcb, wk_ref[hd])
            kb = kf.astype(BF16)

            w_intra = jnp.exp(jnp.where(causal, r_row[hc, :] - big_m[:, hc], -jnp.inf))
            s = _dot_nt(qb, kb) * w_intra
            w_inter = w_inter_all[:, hc]
            c_old = c_scr[bi, hd]
            n_old = n_scr[bi, hc, :]
            num = _dot(s.astype(BF16), vb) + w_inter * _dot(qb, c_old.astype(BF16))
            den = (jnp.sum(s, axis=1, keepdims=True) +
                   w_inter * jnp.sum(qf * n_old, axis=1, keepdims=True))
            hout = num / jnp.maximum(jnp.abs(den), floor_all[:, hc])

            kw = kscale_all[:, hc] * kf
            decay = decay_all[:, hc]
            c_scr[bi, hd] = decay * c_old + _dot_tn(kw.astype(BF16), vb)
            n_scr[bi, hc, :] = decay * n_old + jnp.sum(kw, axis=0, keepdims=True)

            oa = proj(("o", hd), lambda: wm_ref[:, 2 * W + hd * M_HDIM:2 * W + (hd + 1) * M_HDIM])
            hm = _sigmoid(oa[rows, :]) * hout
            hm = hm * lax.rsqrt(jnp.mean(hm * hm, axis=-1, keepdims=True) + EPS) * ghead_ref[:, cs]
            hm = hm + skip_ref[:, cs] * xc_h
            za = proj(("z", hd), lambda: wm_ref[:, 3 * W + hd * M_HDIM:3 * W + (hd + 1) * M_HDIM])
            hm_scr[rows, cs] = (hm * _silu(za[rows, :])).astype(BF16)
            yield

        y_m = _dot(hm_scr[rows, :], wout_ref[...])
        o_ref[bi] = (_sigmoid(ga) * y_m).astype(o_ref.dtype)

    _run_staggered([batch_stages(bi) for bi in range(M_BPS)], M_LAG)


def _mlstm_call(x, a_mod, shift_mod, wm, wgr, bgr, tri, gsel, p4, p16, wconv, bconv, wq, wk, ghead, skip,
                wga, wout):
    B, S, D = x.shape
    L, nb = M_CHUNK, M_BPS
    d4, d16 = ATTN_GROUPS[1][1], ATTN_GROUPS[2][1]
    assert S % L == 0 and B % nb == 0 and L % (d16 * 2 * SUBLANES) == 0
    tok = pl.BlockSpec((nb, L, D), lambda b, c: (b, c, 0))
    per_batch = pl.BlockSpec((nb, 1, D), lambda b, c: (b, 0, 0))
    split = lambda d: pl.BlockSpec((nb, d, L // d, D), lambda b, c: (b, 0, c, 0))
    args = (wm, wgr, bgr, tri, gsel, p4, p16, wconv, bconv, wq, wk, ghead, skip, wga, wout)
    return pl.pallas_call(
        _mlstm_kernel,
        grid=(B // nb, S // L),
        in_specs=[tok, per_batch, per_batch] + [_resident(a.shape) for a in args],
        out_specs=[tok, tok, split(d4), split(d16)],
        out_shape=[
            jax.ShapeDtypeStruct((B, S, D), BF16),
            jax.ShapeDtypeStruct((B, S, D), BF16),
            jax.ShapeDtypeStruct((B, d4, S // d4, D), BF16),
            jax.ShapeDtypeStruct((B, d16, S // d16, D), BF16),
        ],
        scratch_shapes=[
            pltpu.VMEM((nb, M_HEADS, M_HDIM, M_HDIM), F32),
            pltpu.VMEM((nb, SUBLANES, M_HDIM), F32),
            pltpu.VMEM((nb, SUBLANES, LANES), F32),
            pltpu.VMEM((nb, L + SUBLANES, M_WIDTH), F32),
            pltpu.VMEM((nb * L, M_WIDTH), BF16),
            pltpu.VMEM(wm.shape, BF16),
            pltpu.VMEM(wga.shape, BF16),
            pltpu.VMEM(wout.shape, BF16),
        ],
        compiler_params=pltpu.CompilerParams(dimension_semantics=("arbitrary", "arbitrary"),
                                             vmem_limit_bytes=VMEM_LIMIT),
        name="mlstm_branch",
    )(x, a_mod, shift_mod, *args)


def _attn_kernel(h_ref, pos_ref, freq_ref, esel_ref, w_in, o_ref, lse_ref, k_scr, v_scr, w_ref):
    n_res, n = h_ref.shape[0], h_ref.shape[1]
    ub = pl.program_id(2)

    @pl.when((pl.program_id(0) == 0) & (pl.program_id(1) == 0) & (ub == 0))
    def _():
        for blk in range(2 * A_HPG):
            cs_ = slice(blk * A_HDIM, (blk + 1) * A_HDIM)
            src = w_in[:, cs_]
            w_ref[:, cs_] = jnp.concatenate([src[:, lo:hi] for lo, hi in _ROPE_PIECES],
                                            axis=1).astype(BF16)
        w_ref[:, 2 * A_GW:3 * A_GW] = w_in[:, 2 * A_GW:3 * A_GW].astype(BF16)

    @pl.when(ub == 0)
    def _():
        k_scr[:, 0:A_SPAN, :] = jnp.zeros((n_res, A_SPAN, A_GW), BF16)
        v_scr[:, 0:A_SPAN, :] = jnp.zeros((n_res, A_SPAN, A_GW), BF16)

    @pl.when(ub > 0)
    def _():
        k_scr[:, 0:A_SPAN, :] = k_scr[:, n:n + A_SPAN, :]
        v_scr[:, 0:A_SPAN, :] = v_scr[:, n:n + A_SPAN, :]

    lane = lax.broadcasted_iota(jnp.int32, (1, LANES), 1)
    rot_lane = (lane % (LANES // 2)) < ROPE_HALF
    p_idx = lax.broadcasted_iota(jnp.int32, (A_SPAN, 2 * A_SPAN), 0)
    c_idx = lax.broadcasted_iota(jnp.int32, (A_SPAN, 2 * A_SPAN), 1)
    band = (c_idx >= p_idx) & (c_idx <= p_idx + A_SPAN)
    lane_o = lax.broadcasted_iota(jnp.int32, (A_SPAN, LANES), 1)
    scale = A_HDIM ** -0.5

    def residue_stages(rr):
        hb = h_ref[rr]

        ang = freq_ref[...] * pos_ref[rr].astype(F32)
        cs = jnp.concatenate([jnp.cos(ang), jnp.sin(ang)], axis=0)
        cs_hi = cs.astype(BF16)
        cs_lo = (cs - cs_hi.astype(F32)).astype(BF16)
        tab = _dot_tn(jnp.concatenate([cs_hi, cs_lo], axis=0), esel_ref[...])
        q = _dot(hb, w_ref[:, 0:A_GW])
        k = _dot(hb, w_ref[:, A_GW:2 * A_GW])
        yield

        cos_t = tab[:, 0:LANES] + jnp.where(rot_lane, 0.0, 1.0)
        sin_t = tab[:, LANES:2 * LANES]

        def rope(t):
            return t * cos_t + pltpu.roll(t, LANES // 2, 1) * sin_t

        qs = []
        for hd in range(A_HPG):
            cs_ = slice(hd * A_HDIM, (hd + 1) * A_HDIM)
            qs.append((rope(q[:, cs_]) * scale).astype(BF16))
            k_scr[rr, A_SPAN:A_SPAN + n, cs_] = rope(k[:, cs_]).astype(BF16)
        v_scr[rr, A_SPAN:A_SPAN + n, :] = _dot(hb, w_ref[:, 2 * A_GW:3 * A_GW]).astype(BF16)
        yield

        for jb in range(n // A_SPAN):
            rows = slice(jb * A_SPAN, (jb + 1) * A_SPAN)
            krows = slice(jb * A_SPAN, (jb + 2) * A_SPAN)
            if jb == 0:
                valid = band & (c_idx >= jnp.where(ub == 0, A_SPAN, 0))
            else:
                valid = band
            lse_blk = jnp.zeros((A_SPAN, LANES), F32)
            for hd in range(A_HPG):
                cs_ = slice(hd * A_HDIM, (hd + 1) * A_HDIM)
                s = _dot_nt(qs[hd][rows, :], k_scr[rr, krows, cs_])
                s = jnp.where(valid, s, -jnp.inf)
                mx = jnp.max(s, axis=1, keepdims=True)
                e = jnp.exp(s - mx)
                den = jnp.sum(e, axis=1, keepdims=True)
                o = _dot(e.astype(BF16), v_scr[rr, krows, cs_]) / den
                o_ref[rr, rows, cs_] = o.astype(o_ref.dtype)
                lse_blk = jnp.where(lane_o == hd, mx + jnp.log(den), lse_blk)
                if hd % 2 == 1:
                    yield
            lse_ref[rr, rows, :] = lse_blk

    _run_staggered([residue_stages(rr) for rr in range(n_res)], A_LAG)


def _attn_call(h_split, positions, freq, esel, w_qkv):
    B, d, U, D = h_split.shape
    n = min(A_TILE, U)
    n_res = min(d, A_TILE // n)
    assert U % n == 0 and n % A_SPAN == 0 and d % n_res == 0
    pos = positions.reshape(B, U, d).transpose(0, 2, 1).reshape(B, d, 1, U)
    tok = lambda w: pl.BlockSpec((None, n_res, n, w), lambda b, r, u: (b, r, u, 0))
    return pl.pallas_call(
        _attn_kernel,
        grid=(B, d // n_res, U // n),
        in_specs=[
            tok(D),
            pl.BlockSpec((None, n_res, 1, n), lambda b, r, u: (b, r, 0, u)),
            _resident(freq.shape), _resident(esel.shape), _resident(w_qkv.shape),
        ],
        out_specs=[tok(A_GW), tok(LANES)],
        out_shape=[
            jax.ShapeDtypeStruct((B, d, U, A_GW), BF16),
            jax.ShapeDtypeStruct((B, d, U, LANES), F32),
        ],
        scratch_shapes=[
            pltpu.VMEM((n_res, A_SPAN + n, A_GW), BF16),
            pltpu.VMEM((n_res, A_SPAN + n, A_GW), BF16),
            pltpu.VMEM(w_qkv.shape, BF16),
        ],
        compiler_params=pltpu.CompilerParams(
            dimension_semantics=("arbitrary", "arbitrary", "arbitrary"),
            vmem_limit_bytes=VMEM_LIMIT),
        name=f"dilated_attn_d{d}",
    )(h_split, pos, freq, esel, w_qkv)


def _final_kernel(x_ref, h_ref, gate_ref, ym_ref, o1_ref, o2_ref, o3_ref, l1_ref, l2_ref, l3_ref,
                  wz_in, wgb_in, woa_in, wout_in, gfin_ref, out_ref, oz_scr, o_scr, l_scr,
                  wz_ref, wgb_ref, woa_ref, wout_ref):
    T = F_TILE

    @pl.when((pl.program_id(0) == 0) & (pl.program_id(1) == 0))
    def _():
        wz_ref[...] = wz_in[...]
        wgb_ref[...] = wgb_in[...]
        woa_ref[...] = woa_in[...]
        wout_ref[...] = wout_in[...]

    subs = [slice(i * F_SUB, (i + 1) * F_SUB) for i in range(T // F_SUB)]

    zbs = [_dot(h_ref[rs, :], wz_ref[...]) for rs in subs]
    gbs = [_dot(h_ref[rs, :], wgb_ref[...]) for rs in subs]

    for gi, (o_ref, l_ref) in enumerate(((o1_ref, l1_ref), (o2_ref, l2_ref), (o3_ref, l3_ref))):
        d = ATTN_GROUPS[gi][1]
        for r in range(d):
            rows = pl.ds(r, T // d, stride=d)
            l_scr[gi, rows, :] = l_ref[r]
            for hd in range(A_HPG):
                o_scr[gi * A_HPG + hd, rows, :] = (
                    o_ref[r, :, hd * A_HDIM:(hd + 1) * A_HDIM].astype(F32))

    y_as = []
    for rs, zb in zip(subs, zbs):
        l1, l2, l3 = l_scr[0, rs, :], l_scr[1, rs, :], l_scr[2, rs, :]
        mx = jnp.maximum(jnp.maximum(l1, l2), l3)
        e1, e2, e3 = jnp.exp(l1 - mx), jnp.exp(l2 - mx), jnp.exp(l3 - mx)
        inv = 1.0 / (e1 + e2 + e3)
        w1, w2, w3 = e1 * inv, e2 * inv, e3 * inv
        for hd in range(A_HPG):
            cs = slice(hd * A_HDIM, (hd + 1) * A_HDIM)
            hc = slice(hd, hd + 1)
            o_a = (w1[:, hc] * o_scr[hd, rs, :] + w2[:, hc] * o_scr[A_HPG + hd, rs, :] +
                   w3[:, hc] * o_scr[2 * A_HPG + hd, rs, :])
            oz_scr[rs, cs] = (o_a * _silu(zb[:, cs])).astype(BF16)
        y_as.append(_dot(oz_scr[rs, :], woa_ref[...]))

    deltas = []
    for rs, gb, y_a in zip(subs, gbs, y_as):
        merged = ym_ref[rs, :].astype(F32) + _sigmoid(gb) * y_a
        deltas.append(_dot(merged.astype(BF16), wout_ref[...]))

    for rs, delta in zip(subs, deltas):
        xn = x_ref[rs, :] + gate_ref[...] * delta
        out_ref[rs, :] = (xn * lax.rsqrt(jnp.mean(xn * xn, axis=-1, keepdims=True) + EPS) *
                          gfin_ref[...])


def _final_call(x, h_nat, gate_mod, ym, os_, ls_, wz, wgb, woa, wout, gfin):
    B, S, D = x.shape
    T = F_TILE
    assert S % T == 0 and all(T % (d * 2 * SUBLANES) == 0 for _, d in ATTN_GROUPS)
    tok = lambda w: pl.BlockSpec((None, T, w), lambda b, t: (b, t, 0))
    split = lambda a: pl.BlockSpec((None, a.shape[1], T // a.shape[1], a.shape[3]),
                                   lambda b, t: (b, 0, t, 0))
    per_batch = pl.BlockSpec((None, 1, D), lambda b, t: (b, 0, 0))
    wts = (wz, wgb, woa, wout, gfin)
    return pl.pallas_call(
        _final_kernel,
        grid=(B, S // T),
        in_specs=([tok(D), tok(D), per_batch, tok(D)] + [split(a) for a in os_] +
                  [split(a) for a in ls_] + [_resident(a.shape) for a in wts]),
        out_specs=tok(D),
        out_shape=jax.ShapeDtypeStruct((B, S, D), F32),
        scratch_shapes=[
            pltpu.VMEM((T, A_GW), BF16),
            pltpu.VMEM((N_GROUPS * A_HPG, T, A_HDIM), F32),
            pltpu.VMEM((N_GROUPS, T, LANES), F32),
            pltpu.VMEM(wz.shape, BF16),
            pltpu.VMEM(wgb.shape, BF16),
            pltpu.VMEM(woa.shape, BF16),
            pltpu.VMEM(wout.shape, BF16),
        ],
        compiler_params=pltpu.CompilerParams(dimension_semantics=("arbitrary", "arbitrary"),
                                             vmem_limit_bytes=VMEM_LIMIT),
        name="merge_out_norm",
    )(x, h_nat, gate_mod, ym, *os_, *ls_, *wts)


def _rope_tables():
    inv_freq = ROPE_THETA ** (-jnp.arange(ROPE_HALF, dtype=F32) / ROPE_HALF)
    e = np.zeros((4 * ROPE_HALF, 2 * LANES), np.float32)
    for part in range(2):
        for i in range(ROPE_HALF):
            rc, rs = part * 2 * ROPE_HALF + i, part * 2 * ROPE_HALF + ROPE_HALF + i
            e[rc, i] = 1.0
            e[rc, LANES // 2 + i] = 1.0
            e[rs, LANES + i] = -1.0
            e[rs, LANES + LANES // 2 + i] = 1.0
    return inv_freq.reshape(ROPE_HALF, 1), jnp.asarray(e, BF16)


def _gate_select():
    e = np.zeros((_GATE_STACK_ROWS, 2 * LANES), np.float32)
    for q in range(2):
        for part in range(_N_PARTS):
            for hd in range(M_HEADS):
                e[(q * _N_PARTS + part) * SUBLANES + hd, q * LANES + hd] = 1.0
    return jnp.asarray(e, BF16)


def _cumsum_triangle():
    return jnp.asarray(np.triu(np.ones((M_CHUNK, M_CHUNK), np.float32)), BF16)


def _split_perm(d):
    p = np.zeros((M_CHUNK, M_CHUNK), np.float32)
    out = np.arange(M_CHUNK)
    p[out, (out % (M_CHUNK // d)) * d + out // (M_CHUNK // d)] = 1.0
    return jnp.asarray(p, BF16)


def kernel(x, c, positions, w_ada, b_ada, g_norm, w_in, b_igate, b_fgate, w_conv, b_conv,
           w_q_mlstm, w_k_mlstm, g_mlstm_head, skip_mlstm, w_out_mlstm, w_out_attn, w_out, g_final):
    B, S, D = x.shape
    assert D == D_MODEL and w_ada.shape[0] == 1
    W = M_WIDTH
    QKV = N_GROUPS * A_GW
    o_gate = 4 * W
    o_q = o_gate + 2 * M_HEADS
    o_k, o_v = o_q + QKV, o_q + 2 * QKV
    o_zb = o_q + 3 * QKV
    o_ga = o_zb + A_GW
    o_gb = o_ga + D
    w = w_in[0]

    mod = _mod_call(c, w_ada[0], b_ada[0], g_norm[0])
    shift_mod = mod[0].reshape(B, 1, D)
    a_mod = mod[1].reshape(B, 1, D)
    gate_mod = mod[2].reshape(B, 1, D)

    w_i = w[:, o_gate:o_gate + M_HEADS]
    w_f = w[:, o_gate + M_HEADS:o_gate + 2 * M_HEADS]
    pad_r = jnp.zeros((SUBLANES - M_HEADS, D), F32)
    wgr = jnp.concatenate([w_i.T, pad_r, w_f.T, pad_r], axis=0).astype(BF16)
    zr = jnp.zeros((SUBLANES - M_HEADS,), F32)
    bgr = jnp.concatenate([b_igate[0], zr, b_fgate[0], zr]).reshape(2 * SUBLANES, 1)
    ym, h_nat, h_d4, h_d16 = _mlstm_call(
        x, a_mod, shift_mod, w[:, 0:4 * W].astype(BF16), wgr, bgr, _cumsum_triangle(), _gate_select(),
        _split_perm(ATTN_GROUPS[1][1]), _split_perm(ATTN_GROUPS[2][1]),
        w_conv[0], b_conv[0].reshape(1, W), w_q_mlstm[0].astype(BF16),
        (w_k_mlstm[0] * (M_HDIM ** -0.5)).astype(BF16),
        g_mlstm_head[0].reshape(1, W), skip_mlstm[0].reshape(1, W),
        w[:, o_ga:o_ga + D].astype(BF16), w_out_mlstm[0].astype(BF16))

    freq, esel = _rope_tables()
    os_, ls_ = [], []
    for gi, h_split in enumerate((h_nat.reshape(B, 1, S, D), h_d4, h_d16)):
        assert h_split.shape[1] == ATTN_GROUPS[gi][1]
        gs = slice(gi * A_GW, (gi + 1) * A_GW)
        w_qkv = jnp.concatenate([w[:, o_q:o_k][:, gs], w[:, o_k:o_v][:, gs], w[:, o_v:o_zb][:, gs]],
                                axis=1)
        o_g, l_g = _attn_call(h_split, positions, freq, esel, w_qkv)
        os_.append(o_g)
        ls_.append(l_g)

    return _final_call(
        x, h_nat, gate_mod, ym, os_, ls_,
        w[:, o_zb:o_zb + A_GW].astype(BF16), w[:, o_gb:o_gb + D].astype(BF16),
        w_out_attn[0].astype(BF16), w_out[0].astype(BF16), g_final.reshape(1, D))
```

```python
import jax
import jax.numpy as jnp
import numpy as np
from jax import lax
from jax.experimental import pallas as pl
from jax.experimental.pallas import tpu as pltpu

F32 = jnp.float32
BF16 = jnp.bfloat16

D_MODEL = 1024
EPS = 1e-6
M_HEADS = 4
M_HDIM = 256
M_WIDTH = M_HEADS * M_HDIM
CONV_W = 4
M_CHUNK = 256
M_BPS = 2
M_LAG = 5
H_LAG = 2
ATTN_GROUPS = ((128, 1), (512, 4), (2048, 16))
N_GROUPS = len(ATTN_GROUPS)
A_HPG = 4
A_HDIM = 128
A_GW = A_HPG * A_HDIM
A_SPAN = 128
ROPE_DIM = A_HDIM // 4
ROPE_HALF = ROPE_DIM // 2
ROPE_THETA = 500000.0
A_TILE = 1024
A_LAG = 2
F_TILE = 512
F_SUB = 256
LANES = 128
SUBLANES = 8
VMEM_LIMIT = 56 * 1024 * 1024

assert all(w // d == A_SPAN for w, d in ATTN_GROUPS)
_ROPE_PIECES = ((0, ROPE_HALF), (ROPE_DIM, LANES // 2 + ROPE_HALF), (ROPE_HALF, ROPE_DIM),
                (LANES // 2 + ROPE_HALF, A_HDIM))


def _sigmoid(x):
    return 1.0 / (1.0 + jnp.exp(-x))


def _silu(x):
    return x * _sigmoid(x)


def _log_sigmoid(x):
    return jnp.minimum(x, 0.0) - jnp.log(1.0 + jnp.exp(-jnp.abs(x)))


def _mod_norm(x, a_row, shift_row):
    ms = jnp.mean(x * x, axis=-1, keepdims=True)
    return x * lax.rsqrt(ms + EPS) * a_row + shift_row


def _dot(a, b):
    return jnp.dot(a, b, preferred_element_type=F32)


def _dot_nt(a, b):
    return lax.dot_general(a, b, (((1,), (1,)), ((), ())), preferred_element_type=F32)


def _dot_tn(a, b):
    return lax.dot_general(a, b, (((0,), (0,)), ((), ())), preferred_element_type=F32)


def _staggered(stage_generators, lag):
    pending, live, rnd = list(stage_generators), [], 0
    while pending or live:
        if pending and rnd % lag == 0:
            live.append(pending.pop(0))
        for g in list(live):
            try:
                next(g)
            except StopIteration:
                live.remove(g)
        rnd += 1
        yield


def _run_staggered(stage_generators, lag):
    for _ in _staggered(stage_generators, lag):
        pass


def _resident(shape):
    zeros = (0,) * len(shape)
    return pl.BlockSpec(shape, lambda *_: zeros, pipeline_mode=pl.Buffered(1))


def _mod_kernel(c_ref, w_ref, b_ref, g_ref, o_ref):
    j = pl.program_id(0)
    val = _dot(_silu(c_ref[...]), w_ref[...]) + b_ref[...]
    o_ref[...] = jnp.where(j == 1, g_ref[...] * (1.0 + val), val)


def _mod_call(c, w_ada, b_ada, g_norm):
    B = c.shape[0]
    return pl.pallas_call(
        _mod_kernel,
        grid=(3,),
        in_specs=[
            pl.BlockSpec((B, D_MODEL), lambda j: (0, 0)),
            pl.BlockSpec((D_MODEL, D_MODEL), lambda j: (0, j)),
            pl.BlockSpec((1, D_MODEL), lambda j: (0, j)),
            pl.BlockSpec((1, D_MODEL), lambda j: (0, 0)),
        ],
        out_specs=pl.BlockSpec((None, B, D_MODEL), lambda j: (j, 0, 0)),
        out_shape=jax.ShapeDtypeStruct((3, B, D_MODEL), F32),
        compiler_params=pltpu.CompilerParams(dimension_semantics=("arbitrary",),
                                             vmem_limit_bytes=VMEM_LIMIT),
        name="adaln_mod",
    )(c, w_ada, b_ada.reshape(1, 3 * D_MODEL), g_norm.reshape(1, D_MODEL))


def _scan_rows(x, op, identity):
    n = x.shape[0]
    row = lax.broadcasted_iota(jnp.int32, x.shape, 0)
    sh = 1
    while sh < n:
        x = op(x, jnp.where(row >= sh, pltpu.roll(x, sh, 0), identity))
        sh *= 2
    return x


def _bf16_parts(x):
    hi = x.astype(BF16).astype(F32)
    mid = (x - hi).astype(BF16).astype(F32)
    return [hi, mid, x - hi - mid]


_N_PARTS = 3
_CUMSUM_ROWS = 4 * SUBLANES
_GATE_STACK_ROWS = 8 * SUBLANES


def _mlstm_kernel(x_ref, a_ref, shift_ref, wm_in, wgr_ref, bgr_ref, tri_ref, gsel_ref, p4_ref,
                  p16_ref, wconv_ref, bconv_ref, wq_ref, wk_ref, ghead_ref, skip_ref, wga_in,
                  wout_in, o_ref, hnat_ref, hd4_ref, hd16_ref, c_scr, n_scr, m_scr, xa_scr, hm_scr,
                  wm_ref, wga_ref, wout_ref):
    L = M_CHUNK
    W = M_WIDTH
    TAIL = SUBLANES

    @pl.when(pl.program_id(1) == 0)
    def _():
        c_scr[...] = jnp.zeros_like(c_scr)
        n_scr[...] = jnp.zeros_like(n_scr)
        m_scr[...] = jnp.zeros_like(m_scr)
        xa_scr[:, 0:TAIL, :] = jnp.zeros((M_BPS, TAIL, W), F32)

    @pl.when((pl.program_id(0) == 0) & (pl.program_id(1) == 0))
    def _():
        wm_ref[...] = wm_in[...]
        wga_ref[...] = wga_in[...]
        wout_ref[...] = wout_in[...]

    causal = (lax.broadcasted_iota(jnp.int32, (L, L), 0) >=
              lax.broadcasted_iota(jnp.int32, (L, L), 1))

    for bi in range(M_BPS):
        hnat_ref[bi] = _mod_norm(x_ref[bi], a_ref[bi], shift_ref[bi]).astype(BF16)
    hb_all = hnat_ref[...].reshape(M_BPS * L, D_MODEL)
    shared = {}

    def proj(name, w_cols):
        if name not in shared:
            shared[name] = _dot(hb_all, w_cols())
        return shared[name]

    def batch_stages(bi):
        rows = slice(bi * L, (bi + 1) * L)
        hb = hnat_ref[bi]

        def split_to(p_ref, ref):
            d = ref.shape[1]
            perm = _dot(p_ref[...], hb).astype(BF16)
            for r in range(d):
                ref[bi, r] = perm[r * (L // d):(r + 1) * (L // d), :]

        xa_scr[bi, TAIL:TAIL + L, :] = proj("xa", lambda: wm_ref[:, 0:W])[rows, :]
        gr = _dot_nt(wgr_ref[...], hb) + bgr_ref[...]
        yield
        split_to(p4_ref, hd4_ref)
        lf = _bf16_parts(_log_sigmoid(gr[SUBLANES:2 * SUBLANES, :]))
        lf.append(jnp.zeros((_CUMSUM_ROWS - _N_PARTS * SUBLANES, L), F32))
        csum = _dot(jnp.concatenate(lf, axis=0).astype(BF16), tri_ref[...])
        yield
        split_to(p16_ref, hd16_ref)
        a_row = csum[0:SUBLANES] + csum[SUBLANES:2 * SUBLANES] + csum[2 * SUBLANES:3 * SUBLANES]
        r_row = gr[0:SUBLANES, :] - a_row
        stack = _bf16_parts(r_row) + _bf16_parts(a_row)
        stack.append(jnp.zeros((_GATE_STACK_ROWS - 2 * _N_PARTS * SUBLANES, L), F32))
        rc_ac = _dot_tn(jnp.concatenate(stack, axis=0).astype(BF16), gsel_ref[...])
        ga = proj("ga", lambda: wga_ref[...])[rows, :]
        yield

        xcs, vbs = [], []
        for hd in range(M_HEADS):
            cs = slice(hd * M_HDIM, (hd + 1) * M_HDIM)
            conv = bconv_ref[:, cs]
            for j in range(CONV_W):
                off = TAIL - (CONV_W - 1) + j
                conv = conv + wconv_ref[j:j + 1, cs] * xa_scr[bi, off:off + L, cs]
            xcs.append(_silu(conv))
            v_all = proj(("v", hd), lambda: wm_ref[:, W + hd * M_HDIM:W + (hd + 1) * M_HDIM])
            vbs.append(v_all[rows, :].astype(BF16))
            if hd == M_HEADS - 1:
                xa_scr[bi, 0:TAIL, :] = xa_scr[bi, L:L + TAIL, :]
            yield

        r_col, a_col = rc_ac[:, 0:LANES], rc_ac[:, LANES:2 * LANES]
        m_prev = m_scr[bi, 0:1, :]
        big_m = jnp.maximum(_scan_rows(r_col, jnp.maximum, -jnp.inf), m_prev)
        m_last = big_m[L - 1:L, :]
        w_inter_all = jnp.exp(m_prev - big_m)
        floor_all = jnp.exp(-a_col - big_m)
        kscale_all = jnp.exp(r_col - m_last)
        decay_all = jnp.exp(m_prev - m_last)
        m_scr[bi, 0:1, :] = a_col[L - 1:L, :] + m_last
        yield

        def head_stages(hd):
            cs = slice(hd * M_HDIM, (hd + 1) * M_HDIM)
            hc = slice(hd, hd + 1)
            xc_h, vb = xcs[hd], vbs[hd]
            xcb = xc_h.astype(BF16)
            qf = _dot(xcb, wq_ref[hd])
            qb = qf.astype(BF16)
            kf = _dot(xcb, wk_ref[hd])
            kb = kf.astype(BF16)
            w_intra = jnp.exp(jnp.where(causal, r_row[hc, :] - big_m[:, hc], -jnp.inf))
            yield

            s = _dot_nt(qb, kb) * w_intra
            yield

            w_inter = w_inter_all[:, hc]
            c_old = c_scr[bi, hd]
            n_old = n_scr[bi, hc, :]
            num = _dot(s.astype(BF16), vb) + w_inter * _dot(qb, c_old.astype(BF16))
            den = (jnp.sum(s, axis=1, keepdims=True) +
                   w_inter * jnp.sum(qf * n_old, axis=1, keepdims=True))
            hout = num / jnp.maximum(jnp.abs(den), floor_all[:, hc])
            yield

            kw = kscale_all[:, hc] * kf
            decay = decay_all[:, hc]
            c_scr[bi, hd] = decay * c_old + _dot_tn(kw.astype(BF16), vb)
            n_scr[bi, hc, :] = decay * n_old + jnp.sum(kw, axis=0, keepdims=True)
            yield

            oa = proj(("o", hd), lambda: wm_ref[:, 2 * W + hd * M_HDIM:2 * W + (hd + 1) * M_HDIM])
            hm = _sigmoid(oa[rows, :]) * hout
            hm = hm * lax.rsqrt(jnp.mean(hm * hm, axis=-1, keepdims=True) + EPS) * ghead_ref[:, cs]
            hm = hm + skip_ref[:, cs] * xc_h
            yield

            za = proj(("z", hd), lambda: wm_ref[:, 3 * W + hd * M_HDIM:3 * W + (hd + 1) * M_HDIM])
            hm_scr[rows, cs] = (hm * _silu(za[rows, :])).astype(BF16)

        yield from _staggered([head_stages(hd) for hd in range(M_HEADS)], H_LAG)

        y_m = _dot(hm_scr[rows, :], wout_ref[...])
        o_ref[bi] = (_sigmoid(ga) * y_m).astype(o_ref.dtype)

    _run_staggered([batch_stages(bi) for bi in range(M_BPS)], M_LAG)


def _mlstm_call(x, a_mod, shift_mod, wm, wgr, bgr, tri, gsel, p4, p16, wconv, bconv, wq, wk, ghead, skip,
                wga, wout):
    B, S, D = x.shape
    L, nb = M_CHUNK, M_BPS
    d4, d16 = ATTN_GROUPS[1][1], ATTN_GROUPS[2][1]
    assert S % L == 0 and B % nb == 0 and L % (d16 * 2 * SUBLANES) == 0
    tok = pl.BlockSpec((nb, L, D), lambda b, c: (b, c, 0))
    per_batch = pl.BlockSpec((nb, 1, D), lambda b, c: (b, 0, 0))
    split = lambda d: pl.BlockSpec((nb, d, L // d, D), lambda b, c: (b, 0, c, 0))
    args = (wm, wgr, bgr, tri, gsel, p4, p16, wconv, bconv, wq, wk, ghead, skip, wga, wout)
    return pl.pallas_call(
        _mlstm_kernel,
        grid=(B // nb, S // L),
        in_specs=[tok, per_batch, per_batch] + [_resident(a.shape) for a in args],
        out_specs=[tok, tok, split(d4), split(d16)],
        out_shape=[
            jax.ShapeDtypeStruct((B, S, D), BF16),
            jax.ShapeDtypeStruct((B, S, D), BF16),
            jax.ShapeDtypeStruct((B, d4, S // d4, D), BF16),
            jax.ShapeDtypeStruct((B, d16, S // d16, D), BF16),
        ],
        scratch_shapes=[
            pltpu.VMEM((nb, M_HEADS, M_HDIM, M_HDIM), F32),
            pltpu.VMEM((nb, SUBLANES, M_HDIM), F32),
            pltpu.VMEM((nb, SUBLANES, LANES), F32),
            pltpu.VMEM((nb, L + SUBLANES, M_WIDTH), F32),
            pltpu.VMEM((nb * L, M_WIDTH), BF16),
            pltpu.VMEM(wm.shape, BF16),
            pltpu.VMEM(wga.shape, BF16),
            pltpu.VMEM(wout.shape, BF16),
        ],
        compiler_params=pltpu.CompilerParams(dimension_semantics=("arbitrary", "arbitrary"),
                                             vmem_limit_bytes=VMEM_LIMIT),
        name="mlstm_branch",
    )(x, a_mod, shift_mod, *args)


def _attn_kernel(h_ref, pos_ref, freq_ref, esel_ref, w_in, o_ref, lse_ref, k_scr, v_scr, w_ref):
    n_res, n = h_ref.shape[0], h_ref.shape[1]
    ub = pl.program_id(2)

    @pl.when((pl.program_id(0) == 0) & (pl.program_id(1) == 0) & (ub == 0))
    def _():
        for blk in range(2 * A_HPG):
            cs_ = slice(blk * A_HDIM, (blk + 1) * A_HDIM)
            src = w_in[:, cs_]
            w_ref[:, cs_] = jnp.concatenate([src[:, lo:hi] for lo, hi in _ROPE_PIECES],
                                            axis=1).astype(BF16)
        w_ref[:, 2 * A_GW:3 * A_GW] = w_in[:, 2 * A_GW:3 * A_GW].astype(BF16)

    @pl.when(ub == 0)
    def _():
        k_scr[:, 0:A_SPAN, :] = jnp.zeros((n_res, A_SPAN, A_GW), BF16)
        v_scr[:, 0:A_SPAN, :] = jnp.zeros((n_res, A_SPAN, A_GW), BF16)

    @pl.when(ub > 0)
    def _():
        k_scr[:, 0:A_SPAN, :] = k_scr[:, n:n + A_SPAN, :]
        v_scr[:, 0:A_SPAN, :] = v_scr[:, n:n + A_SPAN, :]

    lane = lax.broadcasted_iota(jnp.int32, (1, LANES), 1)
    rot_lane = (lane % (LANES // 2)) < ROPE_HALF
    p_idx = lax.broadcasted_iota(jnp.int32, (A_SPAN, 2 * A_SPAN), 0)
    c_idx = lax.broadcasted_iota(jnp.int32, (A_SPAN, 2 * A_SPAN), 1)
    band = (c_idx >= p_idx) & (c_idx <= p_idx + A_SPAN)
    lane_o = lax.broadcasted_iota(jnp.int32, (A_SPAN, LANES), 1)
    scale = A_HDIM ** -0.5

    def residue_stages(rr):
        hb = h_ref[rr]

        ang = freq_ref[...] * pos_ref[rr].astype(F32)
        cs = jnp.concatenate([jnp.cos(ang), jnp.sin(ang)], axis=0)
        cs_hi = cs.astype(BF16)
        cs_lo = (cs - cs_hi.astype(F32)).astype(BF16)
        tab = _dot_tn(jnp.concatenate([cs_hi, cs_lo], axis=0), esel_ref[...])
        q = _dot(hb, w_ref[:, 0:A_GW])
        k = _dot(hb, w_ref[:, A_GW:2 * A_GW])
        yield

        cos_t = tab[:, 0:LANES] + jnp.where(rot_lane, 0.0, 1.0)
        sin_t = tab[:, LANES:2 * LANES]

        def rope(t):
            return t * cos_t + pltpu.roll(t, LANES // 2, 1) * sin_t

        qs = []
        for hd in range(A_HPG):
            cs_ = slice(hd * A_HDIM, (hd + 1) * A_HDIM)
            qs.append((rope(q[:, cs_]) * scale).astype(BF16))
            k_scr[rr, A_SPAN:A_SPAN + n, cs_] = rope(k[:, cs_]).astype(BF16)
        v_scr[rr, A_SPAN:A_SPAN + n, :] = _dot(hb, w_ref[:, 2 * A_GW:3 * A_GW]).astype(BF16)
        yield

        for jb in range(n // A_SPAN):
            rows = slice(jb * A_SPAN, (jb + 1) * A_SPAN)
            krows = slice(jb * A_SPAN, (jb + 2) * A_SPAN)
            if jb == 0:
                valid = band & (c_idx >= jnp.where(ub == 0, A_SPAN, 0))
            else:
                valid = band
            lse_blk = jnp.zeros((A_SPAN, LANES), F32)
            for hd in range(A_HPG):
                cs_ = slice(hd * A_HDIM, (hd + 1) * A_HDIM)
                s = _dot_nt(qs[hd][rows, :], k_scr[rr, krows, cs_])
                s = jnp.where(valid, s, -jnp.inf)
                mx = jnp.max(s, axis=1, keepdims=True)
                e = jnp.exp(s - mx)
                den = jnp.sum(e, axis=1, keepdims=True)
                o = _dot(e.astype(BF16), v_scr[rr, krows, cs_]) / den
                o_ref[rr, rows, cs_] = o.astype(o_ref.dtype)
                lse_blk = jnp.where(lane_o == hd, mx + jnp.log(den), lse_blk)
                if hd % 2 == 1:
                    yield
            lse_ref[rr, rows, :] = lse_blk

    _run_staggered([residue_stages(rr) for rr in range(n_res)], A_LAG)


def _attn_call(h_split, positions, freq, esel, w_qkv):
    B, d, U, D = h_split.shape
    n = min(A_TILE, U)
    n_res = min(d, A_TILE // n)
    assert U % n == 0 and n % A_SPAN == 0 and d % n_res == 0
    pos = positions.reshape(B, U, d).transpose(0, 2, 1).reshape(B, d, 1, U)
    tok = lambda w: pl.BlockSpec((None, n_res, n, w), lambda b, r, u: (b, r, u, 0))
    return pl.pallas_call(
        _attn_kernel,
        grid=(B, d // n_res, U // n),
        in_specs=[
            tok(D),
            pl.BlockSpec((None, n_res, 1, n), lambda b, r, u: (b, r, 0, u)),
            _resident(freq.shape), _resident(esel.shape), _resident(w_qkv.shape),
        ],
        out_specs=[tok(A_GW), tok(LANES)],
        out_shape=[
            jax.ShapeDtypeStruct((B, d, U, A_GW), BF16),
            jax.ShapeDtypeStruct((B, d, U, LANES), F32),
        ],
        scratch_shapes=[
            pltpu.VMEM((n_res, A_SPAN + n, A_GW), BF16),
            pltpu.VMEM((n_res, A_SPAN + n, A_GW), BF16),
            pltpu.VMEM(w_qkv.shape, BF16),
        ],
        compiler_params=pltpu.CompilerParams(
            dimension_semantics=("arbitrary", "arbitrary", "arbitrary"),
            vmem_limit_bytes=VMEM_LIMIT),
        name=f"dilated_attn_d{d}",
    )(h_split, pos, freq, esel, w_qkv)


def _final_kernel(x_ref, h_ref, gate_ref, ym_ref, o1_ref, o2_ref, o3_ref, l1_ref, l2_ref, l3_ref,
                  wz_in, wgb_in, woa_in, wout_in, gfin_ref, out_ref, oz_scr, o_scr, l_scr,
                  wz_ref, wgb_ref, woa_ref, wout_ref):
    T = F_TILE

    @pl.when((pl.program_id(0) == 0) & (pl.program_id(1) == 0))
    def _():
        wz_ref[...] = wz_in[...]
        wgb_ref[...] = wgb_in[...]
        woa_ref[...] = woa_in[...]
        wout_ref[...] = wout_in[...]

    subs = [slice(i * F_SUB, (i + 1) * F_SUB) for i in range(T // F_SUB)]

    zbs = [_dot(h_ref[rs, :], wz_ref[...]) for rs in subs]
    gbs = [_dot(h_ref[rs, :], wgb_ref[...]) for rs in subs]

    for gi, (o_ref, l_ref) in enumerate(((o1_ref, l1_ref), (o2_ref, l2_ref), (o3_ref, l3_ref))):
        d = ATTN_GROUPS[gi][1]
        for r in range(d):
            rows = pl.ds(r, T // d, stride=d)
            l_scr[gi, rows, :] = l_ref[r]
            for hd in range(A_HPG):
                o_scr[gi * A_HPG + hd, rows, :] = (
                    o_ref[r, :, hd * A_HDIM:(hd + 1) * A_HDIM].astype(F32))

    y_as = []
    for rs, zb in zip(subs, zbs):
        l1, l2, l3 = l_scr[0, rs, :], l_scr[1, rs, :], l_scr[2, rs, :]
        mx = jnp.maximum(jnp.maximum(l1, l2), l3)
        e1, e2, e3 = jnp.exp(l1 - mx), jnp.exp(l2 - mx), jnp.exp(l3 - mx)
        inv = 1.0 / (e1 + e2 + e3)
        w1, w2, w3 = e1 * inv, e2 * inv, e3 * inv
        for hd in range(A_HPG):
            cs = slice(hd * A_HDIM, (hd + 1) * A_HDIM)
            hc = slice(hd, hd + 1)
            o_a = (w1[:, hc] * o_scr[hd, rs, :] + w2[:, hc] * o_scr[A_HPG + hd, rs, :] +
                   w3[:, hc] * o_scr[2 * A_HPG + hd, rs, :])
            oz_scr[rs, cs] = (o_a * _silu(zb[:, cs])).astype(BF16)
        y_as.append(_dot(oz_scr[rs, :], woa_ref[...]))

    deltas = []
    for rs, gb, y_a in zip(subs, gbs, y_as):
        merged = ym_ref[rs, :].astype(F32) + _sigmoid(gb) * y_a
        deltas.append(_dot(merged.astype(BF16), wout_ref[...]))

    for rs, delta in zip(subs, deltas):
        xn = x_ref[rs, :] + gate_ref[...] * delta
        out_ref[rs, :] = (xn * lax.rsqrt(jnp.mean(xn * xn, axis=-1, keepdims=True) + EPS) *
                          gfin_ref[...])


def _final_call(x, h_nat, gate_mod, ym, os_, ls_, wz, wgb, woa, wout, gfin):
    B, S, D = x.shape
    T = F_TILE
    assert S % T == 0 and all(T % (d * 2 * SUBLANES) == 0 for _, d in ATTN_GROUPS)
    tok = lambda w: pl.BlockSpec((None, T, w), lambda b, t: (b, t, 0))
    split = lambda a: pl.BlockSpec((None, a.shape[1], T // a.shape[1], a.shape[3]),
                                   lambda b, t: (b, 0, t, 0))
    per_batch = pl.BlockSpec((None, 1, D), lambda b, t: (b, 0, 0))
    wts = (wz, wgb, woa, wout, gfin)
    return pl.pallas_call(
        _final_kernel,
        grid=(B, S // T),
        in_specs=([tok(D), tok(D), per_batch, tok(D)] + [split(a) for a in os_] +
                  [split(a) for a in ls_] + [_resident(a.shape) for a in wts]),
        out_specs=tok(D),
        out_shape=jax.ShapeDtypeStruct((B, S, D), F32),
        scratch_shapes=[
            pltpu.VMEM((T, A_GW), BF16),
            pltpu.VMEM((N_GROUPS * A_HPG, T, A_HDIM), F32),
            pltpu.VMEM((N_GROUPS, T, LANES), F32),
            pltpu.VMEM(wz.shape, BF16),
            pltpu.VMEM(wgb.shape, BF16),
            pltpu.VMEM(woa.shape, BF16),
            pltpu.VMEM(wout.shape, BF16),
        ],
        compiler_params=pltpu.CompilerParams(dimension_semantics=("arbitrary", "arbitrary"),
                                             vmem_limit_bytes=VMEM_LIMIT),
        name="merge_out_norm",
    )(x, h_nat, gate_mod, ym, *os_, *ls_, *wts)


def _rope_tables():
    inv_freq = ROPE_THETA ** (-jnp.arange(ROPE_HALF, dtype=F32) / ROPE_HALF)
    e = np.zeros((4 * ROPE_HALF, 2 * LANES), np.float32)
    for part in range(2):
        for i in range(ROPE_HALF):
            rc, rs = part * 2 * ROPE_HALF + i, part * 2 * ROPE_HALF + ROPE_HALF + i
            e[rc, i] = 1.0
            e[rc, LANES // 2 + i] = 1.0
            e[rs, LANES + i] = -1.0
            e[rs, LANES + LANES // 2 + i] = 1.0
    return inv_freq.reshape(ROPE_HALF, 1), jnp.asarray(e, BF16)


def _gate_select():
    e = np.zeros((_GATE_STACK_ROWS, 2 * LANES), np.float32)
    for q in range(2):
        for part in range(_N_PARTS):
            for hd in range(M_HEADS):
                e[(q * _N_PARTS + part) * SUBLANES + hd, q * LANES + hd] = 1.0
    return jnp.asarray(e, BF16)


def _cumsum_triangle():
    return jnp.asarray(np.triu(np.ones((M_CHUNK, M_CHUNK), np.float32)), BF16)


def _split_perm(d):
    p = np.zeros((M_CHUNK, M_CHUNK), np.float32)
    out = np.arange(M_CHUNK)
    p[out, (out % (M_CHUNK // d)) * d + out // (M_CHUNK // d)] = 1.0
    return jnp.asarray(p, BF16)


def kernel(x, c, positions, w_ada, b_ada, g_norm, w_in, b_igate, b_fgate, w_conv, b_conv,
           w_q_mlstm, w_k_mlstm, g_mlstm_head, skip_mlstm, w_out_mlstm, w_out_attn, w_out, g_final):
    B, S, D = x.shape
    assert D == D_MODEL and w_ada.shape[0] == 1
    W = M_WIDTH
    QKV = N_GROUPS * A_GW
    o_gate = 4 * W
    o_q = o_gate + 2 * M_HEADS
    o_k, o_v = o_q + QKV, o_q + 2 * QKV
    o_zb = o_q + 3 * QKV
    o_ga = o_zb + A_GW
    o_gb = o_ga + D
    w = w_in[0]

    mod = _mod_call(c, w_ada[0], b_ada[0], g_norm[0])
    shift_mod = mod[0].reshape(B, 1, D)
    a_mod = mod[1].reshape(B, 1, D)
    gate_mod = mod[2].reshape(B, 1, D)

    w_i = w[:, o_gate:o_gate + M_HEADS]
    w_f = w[:, o_gate + M_HEADS:o_gate + 2 * M_HEADS]
    pad_r = jnp.zeros((SUBLANES - M_HEADS, D), F32)
    wgr = jnp.concatenate([w_i.T, pad_r, w_f.T, pad_r], axis=0).astype(BF16)
    zr = jnp.zeros((SUBLANES - M_HEADS,), F32)
    bgr = jnp.concatenate([b_igate[0], zr, b_fgate[0], zr]).reshape(2 * SUBLANES, 1)
    ym, h_nat, h_d4, h_d16 = _mlstm_call(
        x, a_mod, shift_mod, w[:, 0:4 * W].astype(BF16), wgr, bgr, _cumsum_triangle(), _gate_select(),
        _split_perm(ATTN_GROUPS[1][1]), _split_perm(ATTN_GROUPS[2][1]),
        w_conv[0], b_conv[0].reshape(1, W), w_q_mlstm[0].astype(BF16),
        (w_k_mlstm[0] * (M_HDIM ** -0.5)).astype(BF16),
        g_mlstm_head[0].reshape(1, W), skip_mlstm[0].reshape(1, W),
        w[:, o_ga:o_ga + D].astype(BF16), w_out_mlstm[0].astype(BF16))

    freq, esel = _rope_tables()
    os_, ls_ = [], []
    for gi, h_split in enumerate((h_nat.reshape(B, 1, S, D), h_d4, h_d16)):
        assert h_split.shape[1] == ATTN_GROUPS[gi][1]
        gs = slice(gi * A_GW, (gi + 1) * A_GW)
        w_qkv = jnp.concatenate([w[:, o_q:o_k][:, gs], w[:, o_k:o_v][:, gs], w[:, o_v:o_zb][:, gs]],
                                axis=1)
        o_g, l_g = _attn_call(h_split, positions, freq, esel, w_qkv)
        os_.append(o_g)
        ls_.append(l_g)

    return _final_call(
        x, h_nat, gate_mod, ym, os_, ls_,
        w[:, o_zb:o_zb + A_GW].astype(BF16), w[:, o_gb:o_gb + D].astype(BF16),
        w_out_attn[0].astype(BF16), w_out[0].astype(BF16), g_final.reshape(1, D))
```

```python
import jax
import jax.numpy as jnp
import numpy as np
from jax import lax
from jax.experimental import pallas as pl
from jax.experimental.pallas import tpu as pltpu

F32 = jnp.float32
BF16 = jnp.bfloat16

D_MODEL = 1024
EPS = 1e-6
M_HEADS = 4
M_HDIM = 256
M_WIDTH = M_HEADS * M_HDIM
CONV_W = 4
M_CHUNK = 256
M_BPS = 2
M_LAG = 5
H_LAG = 2
ATTN_GROUPS = ((128, 1), (512, 4), (2048, 16))
N_GROUPS = len(ATTN_GROUPS)
A_HPG = 4
A_HDIM = 128
A_GW = A_HPG * A_HDIM
A_SPAN = 128
ROPE_DIM = A_HDIM // 4
ROPE_HALF = ROPE_DIM // 2
ROPE_THETA = 500000.0
A_TILE = 1024
A_LAG = 16
U_LAG = 2
F_TILE = 512
F_SUB = 256
LANES = 128
SUBLANES = 8
VMEM_LIMIT = 56 * 1024 * 1024

assert all(w // d == A_SPAN for w, d in ATTN_GROUPS)
_ROPE_PIECES = ((0, ROPE_HALF), (ROPE_DIM, LANES // 2 + ROPE_HALF), (ROPE_HALF, ROPE_DIM),
                (LANES // 2 + ROPE_HALF, A_HDIM))


def _sigmoid(x):
    return 1.0 / (1.0 + jnp.exp(-x))


def _silu(x):
    return x * _sigmoid(x)


def _log_sigmoid(x):
    return jnp.minimum(x, 0.0) - jnp.log(1.0 + jnp.exp(-jnp.abs(x)))


def _mod_norm(x, a_row, shift_row):
    ms = jnp.mean(x * x, axis=-1, keepdims=True)
    return x * lax.rsqrt(ms + EPS) * a_row + shift_row


def _dot(a, b):
    return jnp.dot(a, b, preferred_element_type=F32)


def _dot_nt(a, b):
    return lax.dot_general(a, b, (((1,), (1,)), ((), ())), preferred_element_type=F32)


def _dot_tn(a, b):
    return lax.dot_general(a, b, (((0,), (0,)), ((), ())), preferred_element_type=F32)


def _staggered(stage_generators, lag):
    pending, live, rnd = list(stage_generators), [], 0
    while pending or live:
        if pending and rnd % lag == 0:
            live.append(pending.pop(0))
        for g in list(live):
            try:
                next(g)
            except StopIteration:
                live.remove(g)
        rnd += 1
        yield


def _run_staggered(stage_generators, lag):
    for _ in _staggered(stage_generators, lag):
        pass


def _resident(shape):
    zeros = (0,) * len(shape)
    return pl.BlockSpec(shape, lambda *_: zeros, pipeline_mode=pl.Buffered(1))


def _mod_kernel(c_ref, w_ref, b_ref, g_ref, o_ref):
    j = pl.program_id(0)
    val = _dot(_silu(c_ref[...]), w_ref[...]) + b_ref[...]
    o_ref[...] = jnp.where(j == 1, g_ref[...] * (1.0 + val), val)


def _mod_call(c, w_ada, b_ada, g_norm):
    B = c.shape[0]
    return pl.pallas_call(
        _mod_kernel,
        grid=(3,),
        in_specs=[
            pl.BlockSpec((B, D_MODEL), lambda j: (0, 0)),
            pl.BlockSpec((D_MODEL, D_MODEL), lambda j: (0, j)),
            pl.BlockSpec((1, D_MODEL), lambda j: (0, j)),
            pl.BlockSpec((1, D_MODEL), lambda j: (0, 0)),
        ],
        out_specs=pl.BlockSpec((None, B, D_MODEL), lambda j: (j, 0, 0)),
        out_shape=jax.ShapeDtypeStruct((3, B, D_MODEL), F32),
        compiler_params=pltpu.CompilerParams(dimension_semantics=("arbitrary",),
                                             vmem_limit_bytes=VMEM_LIMIT),
        name="adaln_mod",
    )(c, w_ada, b_ada.reshape(1, 3 * D_MODEL), g_norm.reshape(1, D_MODEL))


def _scan_rows(x, op, identity):
    n = x.shape[0]
    row = lax.broadcasted_iota(jnp.int32, x.shape, 0)
    sh = 1
    while sh < n:
        x = op(x, jnp.where(row >= sh, pltpu.roll(x, sh, 0), identity))
        sh *= 2
    return x


def _bf16_parts(x):
    hi = x.astype(BF16).astype(F32)
    mid = (x - hi).astype(BF16).astype(F32)
    return [hi, mid, x - hi - mid]


_N_PARTS = 3
_CUMSUM_ROWS = 4 * SUBLANES
_GATE_STACK_ROWS = 8 * SUBLANES


def _mlstm_kernel(x_ref, a_ref, shift_ref, wm_in, wgr_ref, bgr_ref, tri_ref, gsel_ref, p4_ref,
                  p16_ref, wconv_ref, bconv_ref, wq_ref, wk_ref, ghead_ref, skip_ref, wga_in,
                  wout_in, o_ref, hnat_ref, hd4_ref, hd16_ref, c_scr, n_scr, m_scr, xa_scr, hm_scr,
                  wm_ref, wga_ref, wout_ref):
    L = M_CHUNK
    W = M_WIDTH
    TAIL = SUBLANES

    @pl.when(pl.program_id(1) == 0)
    def _():
        c_scr[...] = jnp.zeros_like(c_scr)
        n_scr[...] = jnp.zeros_like(n_scr)
        m_scr[...] = jnp.zeros_like(m_scr)
        xa_scr[:, 0:TAIL, :] = jnp.zeros((M_BPS, TAIL, W), F32)

    @pl.when((pl.program_id(0) == 0) & (pl.program_id(1) == 0))
    def _():
        wm_ref[...] = wm_in[...]
        wga_ref[...] = wga_in[...]
        wout_ref[...] = wout_in[...]

    causal = (lax.broadcasted_iota(jnp.int32, (L, L), 0) >=
              lax.broadcasted_iota(jnp.int32, (L, L), 1))

    for bi in range(M_BPS):
        hnat_ref[bi] = _mod_norm(x_ref[bi], a_ref[bi], shift_ref[bi]).astype(BF16)
    hb_all = hnat_ref[...].reshape(M_BPS * L, D_MODEL)
    shared = {}

    def proj(name, w_cols):
        if name not in shared:
            shared[name] = _dot(hb_all, w_cols())
        return shared[name]

    def batch_stages(bi):
        rows = slice(bi * L, (bi + 1) * L)
        hb = hnat_ref[bi]

        def split_to(p_ref, ref):
            d = ref.shape[1]
            perm = _dot(p_ref[...], hb).astype(BF16)
            for r in range(d):
                ref[bi, r] = perm[r * (L // d):(r + 1) * (L // d), :]

        xa_scr[bi, TAIL:TAIL + L, :] = proj("xa", lambda: wm_ref[:, 0:W])[rows, :]
        gr = _dot_nt(wgr_ref[...], hb) + bgr_ref[...]
        yield
        split_to(p4_ref, hd4_ref)
        lf = _bf16_parts(_log_sigmoid(gr[SUBLANES:2 * SUBLANES, :]))
        lf.append(jnp.zeros((_CUMSUM_ROWS - _N_PARTS * SUBLANES, L), F32))
        csum = _dot(jnp.concatenate(lf, axis=0).astype(BF16), tri_ref[...])
        yield
        split_to(p16_ref, hd16_ref)
        a_row = csum[0:SUBLANES] + csum[SUBLANES:2 * SUBLANES] + csum[2 * SUBLANES:3 * SUBLANES]
        r_row = gr[0:SUBLANES, :] - a_row
        stack = _bf16_parts(r_row) + _bf16_parts(a_row)
        stack.append(jnp.zeros((_GATE_STACK_ROWS - 2 * _N_PARTS * SUBLANES, L), F32))
        rc_ac = _dot_tn(jnp.concatenate(stack, axis=0).astype(BF16), gsel_ref[...])
        ga = proj("ga", lambda: wga_ref[...])[rows, :]
        yield

        xcs, vbs = [], []
        for hd in range(M_HEADS):
            cs = slice(hd * M_HDIM, (hd + 1) * M_HDIM)
            conv = bconv_ref[:, cs]
            for j in range(CONV_W):
                off = TAIL - (CONV_W - 1) + j
                conv = conv + wconv_ref[j:j + 1, cs] * xa_scr[bi, off:off + L, cs]
            xcs.append(_silu(conv))
            v_all = proj(("v", hd), lambda: wm_ref[:, W + hd * M_HDIM:W + (hd + 1) * M_HDIM])
            vbs.append(v_all[rows, :].astype(BF16))
            if hd == M_HEADS - 1:
                xa_scr[bi, 0:TAIL, :] = xa_scr[bi, L:L + TAIL, :]
            yield

        r_col, a_col = rc_ac[:, 0:LANES], rc_ac[:, LANES:2 * LANES]
        m_prev = m_scr[bi, 0:1, :]
        big_m = jnp.maximum(_scan_rows(r_col, jnp.maximum, -jnp.inf), m_prev)
        m_last = big_m[L - 1:L, :]
        w_inter_all = jnp.exp(m_prev - big_m)
        floor_all = jnp.exp(-a_col - big_m)
        kscale_all = jnp.exp(r_col - m_last)
        decay_all = jnp.exp(m_prev - m_last)
        m_scr[bi, 0:1, :] = a_col[L - 1:L, :] + m_last
        yield

        def head_stages(hd):
            cs = slice(hd * M_HDIM, (hd + 1) * M_HDIM)
            hc = slice(hd, hd + 1)
            xc_h, vb = xcs[hd], vbs[hd]
            xcb = xc_h.astype(BF16)
            qf = _dot(xcb, wq_ref[hd])
            qb = qf.astype(BF16)
            kf = _dot(xcb, wk_ref[hd])
            kb = kf.astype(BF16)
            w_intra = jnp.exp(jnp.where(causal, r_row[hc, :] - big_m[:, hc], -jnp.inf))
            yield

            s = _dot_nt(qb, kb) * w_intra
            yield

            w_inter = w_inter_all[:, hc]
            c_old = c_scr[bi, hd]
            n_old = n_scr[bi, hc, :]
            num = _dot(s.astype(BF16), vb) + w_inter * _dot(qb, c_old.astype(BF16))
            den = (jnp.sum(s, axis=1, keepdims=True) +
                   w_inter * jnp.sum(qf * n_old, axis=1, keepdims=True))
            hout = num / jnp.maximum(jnp.abs(den), floor_all[:, hc])
            yield

            kw = kscale_all[:, hc] * kf
            decay = decay_all[:, hc]
            c_scr[bi, hd] = decay * c_old + _dot_tn(kw.astype(BF16), vb)
            n_scr[bi, hc, :] = decay * n_old + jnp.sum(kw, axis=0, keepdims=True)
            yield

            oa = proj(("o", hd), lambda: wm_ref[:, 2 * W + hd * M_HDIM:2 * W + (hd + 1) * M_HDIM])
            hm = _sigmoid(oa[rows, :]) * hout
            hm = hm * lax.rsqrt(jnp.mean(hm * hm, axis=-1, keepdims=True) + EPS) * ghead_ref[:, cs]
            hm = hm + skip_ref[:, cs] * xc_h
            yield

            za = proj(("z", hd), lambda: wm_ref[:, 3 * W + hd * M_HDIM:3 * W + (hd + 1) * M_HDIM])
            hm_scr[rows, cs] = (hm * _silu(za[rows, :])).astype(BF16)

        yield from _staggered([head_stages(hd) for hd in range(M_HEADS)], H_LAG)

        y_m = _dot(hm_scr[rows, :], wout_ref[...])
        o_ref[bi] = (_sigmoid(ga) * y_m).astype(o_ref.dtype)

    _run_staggered([batch_stages(bi) for bi in range(M_BPS)], M_LAG)


def _mlstm_call(x, a_mod, shift_mod, wm, wgr, bgr, tri, gsel, p4, p16, wconv, bconv, wq, wk, ghead, skip,
                wga, wout):
    B, S, D = x.shape
    L, nb = M_CHUNK, M_BPS
    d4, d16 = ATTN_GROUPS[1][1], ATTN_GROUPS[2][1]
    assert S % L == 0 and B % nb == 0 and L % (d16 * 2 * SUBLANES) == 0
    tok = pl.BlockSpec((nb, L, D), lambda b, c: (b, c, 0))
    per_batch = pl.BlockSpec((nb, 1, D), lambda b, c: (b, 0, 0))
    split = lambda d: pl.BlockSpec((nb, d, L // d, D), lambda b, c: (b, 0, c, 0))
    args = (wm, wgr, bgr, tri, gsel, p4, p16, wconv, bconv, wq, wk, ghead, skip, wga, wout)
    return pl.pallas_call(
        _mlstm_kernel,
        grid=(B // nb, S // L),
        in_specs=[tok, per_batch, per_batch] + [_resident(a.shape) for a in args],
        out_specs=[tok, tok, split(d4), split(d16)],
        out_shape=[
            jax.ShapeDtypeStruct((B, S, D), BF16),
            jax.ShapeDtypeStruct((B, S, D), BF16),
            jax.ShapeDtypeStruct((B, d4, S // d4, D), BF16),
            jax.ShapeDtypeStruct((B, d16, S // d16, D), BF16),
        ],
        scratch_shapes=[
            pltpu.VMEM((nb, M_HEADS, M_HDIM, M_HDIM), F32),
            pltpu.VMEM((nb, SUBLANES, M_HDIM), F32),
            pltpu.VMEM((nb, SUBLANES, LANES), F32),
            pltpu.VMEM((nb, L + SUBLANES, M_WIDTH), F32),
            pltpu.VMEM((nb * L, M_WIDTH), BF16),
            pltpu.VMEM(wm.shape, BF16),
            pltpu.VMEM(wga.shape, BF16),
            pltpu.VMEM(wout.shape, BF16),
        ],
        compiler_params=pltpu.CompilerParams(dimension_semantics=("arbitrary", "arbitrary"),
                                             vmem_limit_bytes=VMEM_LIMIT),
        name="mlstm_branch",
    )(x, a_mod, shift_mod, *args)


def _attn_kernel(h_ref, pos_ref, freq_ref, esel_ref, w_in, o_ref, lse_ref, k_scr, v_scr, w_ref):
    n_res, n = h_ref.shape[0], h_ref.shape[1]
    ub = pl.program_id(2)

    @pl.when((pl.program_id(0) == 0) & (pl.program_id(1) == 0) & (ub == 0))
    def _():
        for blk in range(2 * A_HPG):
            cs_ = slice(blk * A_HDIM, (blk + 1) * A_HDIM)
            src = w_in[:, cs_]
            w_ref[:, cs_] = jnp.concatenate([src[:, lo:hi] for lo, hi in _ROPE_PIECES],
                                            axis=1).astype(BF16)
        w_ref[:, 2 * A_GW:3 * A_GW] = w_in[:, 2 * A_GW:3 * A_GW].astype(BF16)

    @pl.when(ub == 0)
    def _():
        k_scr[:, 0:A_SPAN, :] = jnp.zeros((n_res, A_SPAN, A_GW), BF16)
        v_scr[:, 0:A_SPAN, :] = jnp.zeros((n_res, A_SPAN, A_GW), BF16)

    @pl.when(ub > 0)
    def _():
        k_scr[:, 0:A_SPAN, :] = k_scr[:, n:n + A_SPAN, :]
        v_scr[:, 0:A_SPAN, :] = v_scr[:, n:n + A_SPAN, :]

    lane = lax.broadcasted_iota(jnp.int32, (1, LANES), 1)
    rot_lane = (lane % (LANES // 2)) < ROPE_HALF
    p_idx = lax.broadcasted_iota(jnp.int32, (A_SPAN, 2 * A_SPAN), 0)
    c_idx = lax.broadcasted_iota(jnp.int32, (A_SPAN, 2 * A_SPAN), 1)
    band = (c_idx >= p_idx) & (c_idx <= p_idx + A_SPAN)
    lane_o = lax.broadcasted_iota(jnp.int32, (A_SPAN, LANES), 1)
    scale = A_HDIM ** -0.5

    def residue_stages(rr):
        hb = h_ref[rr]

        ang = freq_ref[...] * pos_ref[rr].astype(F32)
        cs = jnp.concatenate([jnp.cos(ang), jnp.sin(ang)], axis=0)
        cs_hi = cs.astype(BF16)
        cs_lo = (cs - cs_hi.astype(F32)).astype(BF16)
        tab = _dot_tn(jnp.concatenate([cs_hi, cs_lo], axis=0), esel_ref[...])
        q = _dot(hb, w_ref[:, 0:A_GW])
        k = _dot(hb, w_ref[:, A_GW:2 * A_GW])
        yield

        cos_t = tab[:, 0:LANES] + jnp.where(rot_lane, 0.0, 1.0)
        sin_t = tab[:, LANES:2 * LANES]

        def rope(t):
            return t * cos_t + pltpu.roll(t, LANES // 2, 1) * sin_t

        qs = []
        for hd in range(A_HPG):
            cs_ = slice(hd * A_HDIM, (hd + 1) * A_HDIM)
            qs.append((rope(q[:, cs_]) * scale).astype(BF16))
            k_scr[rr, A_SPAN:A_SPAN + n, cs_] = rope(k[:, cs_]).astype(BF16)
        v_scr[rr, A_SPAN:A_SPAN + n, :] = _dot(hb, w_ref[:, 2 * A_GW:3 * A_GW]).astype(BF16)
        yield

        first_valid = band & (c_idx >= jnp.where(ub == 0, A_SPAN, 0))
        lses = {}

        def unit_stages(jb, hd):
            rows = slice(jb * A_SPAN, (jb + 1) * A_SPAN)
            krows = slice(jb * A_SPAN, (jb + 2) * A_SPAN)
            cs_ = slice(hd * A_HDIM, (hd + 1) * A_HDIM)
            s = _dot_nt(qs[hd][rows, :], k_scr[rr, krows, cs_])
            yield
            s = jnp.where(first_valid if jb == 0 else band, s, -jnp.inf)
            mx = jnp.max(s, axis=1, keepdims=True)
            yield
            e = jnp.exp(s - mx)
            den = jnp.sum(e, axis=1, keepdims=True)
            eb = e.astype(BF16)
            yield
            o = _dot(eb, v_scr[rr, krows, cs_])
            yield
            o = o / den
            o_ref[rr, rows, cs_] = o.astype(o_ref.dtype)
            lses[jb, hd] = mx + jnp.log(den)
            if hd == A_HPG - 1:
                lse_blk = jnp.zeros((A_SPAN, LANES), F32)
                for h2 in range(A_HPG):
                    lse_blk = jnp.where(lane_o == h2, lses[jb, h2], lse_blk)
                lse_ref[rr, rows, :] = lse_blk

        yield from _staggered([unit_stages(jb, hd) for jb in range(n // A_SPAN)
                               for hd in range(A_HPG)], U_LAG)

    _run_staggered([residue_stages(rr) for rr in range(n_res)], A_LAG)


def _attn_call(h_split, positions, freq, esel, w_qkv):
    B, d, U, D = h_split.shape
    n = min(A_TILE, U)
    n_res = min(d, A_TILE // n)
    assert U % n == 0 and n % A_SPAN == 0 and d % n_res == 0
    pos = positions.reshape(B, U, d).transpose(0, 2, 1).reshape(B, d, 1, U)
    tok = lambda w: pl.BlockSpec((None, n_res, n, w), lambda b, r, u: (b, r, u, 0))
    return pl.pallas_call(
        _attn_kernel,
        grid=(B, d // n_res, U // n),
        in_specs=[
            tok(D),
            pl.BlockSpec((None, n_res, 1, n), lambda b, r, u: (b, r, 0, u)),
            _resident(freq.shape), _resident(esel.shape), _resident(w_qkv.shape),
        ],
        out_specs=[tok(A_GW), tok(LANES)],
        out_shape=[
            jax.ShapeDtypeStruct((B, d, U, A_GW), BF16),
            jax.ShapeDtypeStruct((B, d, U, LANES), F32),
        ],
        scratch_shapes=[
            pltpu.VMEM((n_res, A_SPAN + n, A_GW), BF16),
            pltpu.VMEM((n_res, A_SPAN + n, A_GW), BF16),
            pltpu.VMEM(w_qkv.shape, BF16),
        ],
        compiler_params=pltpu.CompilerParams(
            dimension_semantics=("arbitrary", "arbitrary", "arbitrary"),
            vmem_limit_bytes=VMEM_LIMIT),
        name=f"dilated_attn_d{d}",
    )(h_split, pos, freq, esel, w_qkv)


def _final_kernel(x_ref, h_ref, gate_ref, ym_ref, o1_ref, o2_ref, o3_ref, l1_ref, l2_ref, l3_ref,
                  wz_in, wgb_in, woa_in, wout_in, gfin_ref, out_ref, oz_scr, o_scr, l_scr,
                  wz_ref, wgb_ref, woa_ref, wout_ref):
    T = F_TILE

    @pl.when((pl.program_id(0) == 0) & (pl.program_id(1) == 0))
    def _():
        wz_ref[...] = wz_in[...]
        wgb_ref[...] = wgb_in[...]
        woa_ref[...] = woa_in[...]
        wout_ref[...] = wout_in[...]

    subs = [slice(i * F_SUB, (i + 1) * F_SUB) for i in range(T // F_SUB)]

    zbs = [_dot(h_ref[rs, :], wz_ref[...]) for rs in subs]
    gbs = [_dot(h_ref[rs, :], wgb_ref[...]) for rs in subs]

    for gi, (o_ref, l_ref) in enumerate(((o1_ref, l1_ref), (o2_ref, l2_ref), (o3_ref, l3_ref))):
        d = ATTN_GROUPS[gi][1]
        for r in range(d):
            rows = pl.ds(r, T // d, stride=d)
            l_scr[gi, rows, :] = l_ref[r]
            for hd in range(A_HPG):
                o_scr[gi * A_HPG + hd, rows, :] = (
                    o_ref[r, :, hd * A_HDIM:(hd + 1) * A_HDIM].astype(F32))

    y_as = []
    for rs, zb in zip(subs, zbs):
        l1, l2, l3 = l_scr[0, rs, :], l_scr[1, rs, :], l_scr[2, rs, :]
        mx = jnp.maximum(jnp.maximum(l1, l2), l3)
        e1, e2, e3 = jnp.exp(l1 - mx), jnp.exp(l2 - mx), jnp.exp(l3 - mx)
        inv = 1.0 / (e1 + e2 + e3)
        w1, w2, w3 = e1 * inv, e2 * inv, e3 * inv
        for hd in range(A_HPG):
            cs = slice(hd * A_HDIM, (hd + 1) * A_HDIM)
            hc = slice(hd, hd + 1)
            o_a = (w1[:, hc] * o_scr[hd, rs, :] + w2[:, hc] * o_scr[A_HPG + hd, rs, :] +
                   w3[:, hc] * o_scr[2 * A_HPG + hd, rs, :])
            oz_scr[rs, cs] = (o_a * _silu(zb[:, cs])).astype(BF16)
        y_as.append(_dot(oz_scr[rs, :], woa_ref[...]))

    deltas = []
    for rs, gb, y_a in zip(subs, gbs, y_as):
        merged = ym_ref[rs, :].astype(F32) + _sigmoid(gb) * y_a
        deltas.append(_dot(merged.astype(BF16), wout_ref[...]))

    for rs, delta in zip(subs, deltas):
        xn = x_ref[rs, :] + gate_ref[...] * delta
        out_ref[rs, :] = (xn * lax.rsqrt(jnp.mean(xn * xn, axis=-1, keepdims=True) + EPS) *
                          gfin_ref[...])


def _final_call(x, h_nat, gate_mod, ym, os_, ls_, wz, wgb, woa, wout, gfin):
    B, S, D = x.shape
    T = F_TILE
    assert S % T == 0 and all(T % (d * 2 * SUBLANES) == 0 for _, d in ATTN_GROUPS)
    tok = lambda w: pl.BlockSpec((None, T, w), lambda b, t: (b, t, 0))
    split = lambda a: pl.BlockSpec((None, a.shape[1], T // a.shape[1], a.shape[3]),
                                   lambda b, t: (b, 0, t, 0))
    per_batch = pl.BlockSpec((None, 1, D), lambda b, t: (b, 0, 0))
    wts = (wz, wgb, woa, wout, gfin)
    return pl.pallas_call(
        _final_kernel,
        grid=(B, S // T),
        in_specs=([tok(D), tok(D), per_batch, tok(D)] + [split(a) for a in os_] +
                  [split(a) for a in ls_] + [_resident(a.shape) for a in wts]),
        out_specs=tok(D),
        out_shape=jax.ShapeDtypeStruct((B, S, D), F32),
        scratch_shapes=[
            pltpu.VMEM((T, A_GW), BF16),
            pltpu.VMEM((N_GROUPS * A_HPG, T, A_HDIM), F32),
            pltpu.VMEM((N_GROUPS, T, LANES), F32),
            pltpu.VMEM(wz.shape, BF16),
            pltpu.VMEM(wgb.shape, BF16),
            pltpu.VMEM(woa.shape, BF16),
            pltpu.VMEM(wout.shape, BF16),
        ],
        compiler_params=pltpu.CompilerParams(dimension_semantics=("arbitrary", "arbitrary"),
                                             vmem_limit_bytes=VMEM_LIMIT),
        name="merge_out_norm",
    )(x, h_nat, gate_mod, ym, *os_, *ls_, *wts)


def _rope_tables():
    inv_freq = ROPE_THETA ** (-jnp.arange(ROPE_HALF, dtype=F32) / ROPE_HALF)
    e = np.zeros((4 * ROPE_HALF, 2 * LANES), np.float32)
    for part in range(2):
        for i in range(ROPE_HALF):
            rc, rs = part * 2 * ROPE_HALF + i, part * 2 * ROPE_HALF + ROPE_HALF + i
            e[rc, i] = 1.0
            e[rc, LANES // 2 + i] = 1.0
            e[rs, LANES + i] = -1.0
            e[rs, LANES + LANES // 2 + i] = 1.0
    return inv_freq.reshape(ROPE_HALF, 1), jnp.asarray(e, BF16)


def _gate_select():
    e = np.zeros((_GATE_STACK_ROWS, 2 * LANES), np.float32)
    for q in range(2):
        for part in range(_N_PARTS):
            for hd in range(M_HEADS):
                e[(q * _N_PARTS + part) * SUBLANES + hd, q * LANES + hd] = 1.0
    return jnp.asarray(e, BF16)


def _cumsum_triangle():
    return jnp.asarray(np.triu(np.ones((M_CHUNK, M_CHUNK), np.float32)), BF16)


def _split_perm(d):
    p = np.zeros((M_CHUNK, M_CHUNK), np.float32)
    out = np.arange(M_CHUNK)
    p[out, (out % (M_CHUNK // d)) * d + out // (M_CHUNK // d)] = 1.0
    return jnp.asarray(p, BF16)


def kernel(x, c, positions, w_ada, b_ada, g_norm, w_in, b_igate, b_fgate, w_conv, b_conv,
           w_q_mlstm, w_k_mlstm, g_mlstm_head, skip_mlstm, w_out_mlstm, w_out_attn, w_out, g_final):
    B, S, D = x.shape
    assert D == D_MODEL and w_ada.shape[0] == 1
    W = M_WIDTH
    QKV = N_GROUPS * A_GW
    o_gate = 4 * W
    o_q = o_gate + 2 * M_HEADS
    o_k, o_v = o_q + QKV, o_q + 2 * QKV
    o_zb = o_q + 3 * QKV
    o_ga = o_zb + A_GW
    o_gb = o_ga + D
    w = w_in[0]

    mod = _mod_call(c, w_ada[0], b_ada[0], g_norm[0])
    shift_mod = mod[0].reshape(B, 1, D)
    a_mod = mod[1].reshape(B, 1, D)
    gate_mod = mod[2].reshape(B, 1, D)

    w_i = w[:, o_gate:o_gate + M_HEADS]
    w_f = w[:, o_gate + M_HEADS:o_gate + 2 * M_HEADS]
    pad_r = jnp.zeros((SUBLANES - M_HEADS, D), F32)
    wgr = jnp.concatenate([w_i.T, pad_r, w_f.T, pad_r], axis=0).astype(BF16)
    zr = jnp.zeros((SUBLANES - M_HEADS,), F32)
    bgr = jnp.concatenate([b_igate[0], zr, b_fgate[0], zr]).reshape(2 * SUBLANES, 1)
    ym, h_nat, h_d4, h_d16 = _mlstm_call(
        x, a_mod, shift_mod, w[:, 0:4 * W].astype(BF16), wgr, bgr, _cumsum_triangle(), _gate_select(),
        _split_perm(ATTN_GROUPS[1][1]), _split_perm(ATTN_GROUPS[2][1]),
        w_conv[0], b_conv[0].reshape(1, W), w_q_mlstm[0].astype(BF16),
        (w_k_mlstm[0] * (M_HDIM ** -0.5)).astype(BF16),
        g_mlstm_head[0].reshape(1, W), skip_mlstm[0].reshape(1, W),
        w[:, o_ga:o_ga + D].astype(BF16), w_out_mlstm[0].astype(BF16))

    freq, esel = _rope_tables()
    os_, ls_ = [], []
    for gi, h_split in enumerate((h_nat.reshape(B, 1, S, D), h_d4, h_d16)):
        assert h_split.shape[1] == ATTN_GROUPS[gi][1]
        gs = slice(gi * A_GW, (gi + 1) * A_GW)
        w_qkv = jnp.concatenate([w[:, o_q:o_k][:, gs], w[:, o_k:o_v][:, gs], w[:, o_v:o_zb][:, gs]],
                                axis=1)
        o_g, l_g = _attn_call(h_split, positions, freq, esel, w_qkv)
        os_.append(o_g)
        ls_.append(l_g)

    return _final_call(
        x, h_nat, gate_mod, ym, os_, ls_,
        w[:, o_zb:o_zb + A_GW].astype(BF16), w[:, o_gb:o_gb + D].astype(BF16),
        w_out_attn[0].astype(BF16), w_out[0].astype(BF16), g_final.reshape(1, D))
```

```python
import jax
import jax.numpy as jnp
import numpy as np
from jax import lax
from jax.experimental import pallas as pl
from jax.experimental.pallas import tpu as pltpu

F32 = jnp.float32
BF16 = jnp.bfloat16

D_MODEL = 1024
EPS = 1e-6
M_HEADS = 4
M_HDIM = 256
M_WIDTH = M_HEADS * M_HDIM
CONV_W = 4
M_CHUNK = 256
M_BPS = 2
M_LAG = 3
H_LAG = 3
ATTN_GROUPS = ((128, 1), (512, 4), (2048, 16))
N_GROUPS = len(ATTN_GROUPS)
A_HPG = 4
A_HDIM = 128
A_GW = A_HPG * A_HDIM
A_SPAN = 128
ROPE_DIM = A_HDIM // 4
ROPE_HALF = ROPE_DIM // 2
ROPE_THETA = 500000.0
A_TILE = 1024
A_SUB = 256
A_LAG = 16
U_LAG = 2
F_TILE = 512
F_SUB = 256
LANES = 128
SUBLANES = 8
VMEM_LIMIT = 56 * 1024 * 1024

assert all(w // d == A_SPAN for w, d in ATTN_GROUPS)
_ROPE_PIECES = ((0, ROPE_HALF), (ROPE_DIM, LANES // 2 + ROPE_HALF), (ROPE_HALF, ROPE_DIM),
                (LANES // 2 + ROPE_HALF, A_HDIM))


def _sigmoid(x):
    return 1.0 / (1.0 + jnp.exp(-x))


def _silu(x):
    return x * _sigmoid(x)


def _log_sigmoid(x):
    return jnp.minimum(x, 0.0) - jnp.log(1.0 + jnp.exp(-jnp.abs(x)))


def _mod_norm(x, a_row, shift_row):
    ms = jnp.mean(x * x, axis=-1, keepdims=True)
    return x * lax.rsqrt(ms + EPS) * a_row + shift_row


def _dot(a, b):
    return jnp.dot(a, b, preferred_element_type=F32)


def _dot_nt(a, b):
    return lax.dot_general(a, b, (((1,), (1,)), ((), ())), preferred_element_type=F32)


def _dot_tn(a, b):
    return lax.dot_general(a, b, (((0,), (0,)), ((), ())), preferred_element_type=F32)


def _staggered(stage_generators, lag):
    pending, live, rnd = list(stage_generators), [], 0
    while pending or live:
        if pending and rnd % lag == 0:
            live.append(pending.pop(0))
        for g in list(live):
            try:
                next(g)
            except StopIteration:
                live.remove(g)
        rnd += 1
        yield


def _run_staggered(stage_generators, lag):
    for _ in _staggered(stage_generators, lag):
        pass


def _resident(shape):
    zeros = (0,) * len(shape)
    return pl.BlockSpec(shape, lambda *_: zeros, pipeline_mode=pl.Buffered(1))


def _mod_kernel(c_ref, w_ref, b_ref, g_ref, o_ref):
    j = pl.program_id(0)
    val = _dot(_silu(c_ref[...]), w_ref[...]) + b_ref[...]
    o_ref[...] = jnp.where(j == 1, g_ref[...] * (1.0 + val), val)


def _mod_call(c, w_ada, b_ada, g_norm):
    B = c.shape[0]
    return pl.pallas_call(
        _mod_kernel,
        grid=(3,),
        in_specs=[
            pl.BlockSpec((B, D_MODEL), lambda j: (0, 0)),
            pl.BlockSpec((D_MODEL, D_MODEL), lambda j: (0, j)),
            pl.BlockSpec((1, D_MODEL), lambda j: (0, j)),
            pl.BlockSpec((1, D_MODEL), lambda j: (0, 0)),
        ],
        out_specs=pl.BlockSpec((None, B, D_MODEL), lambda j: (j, 0, 0)),
        out_shape=jax.ShapeDtypeStruct((3, B, D_MODEL), F32),
        compiler_params=pltpu.CompilerParams(dimension_semantics=("arbitrary",),
                                             vmem_limit_bytes=VMEM_LIMIT),
        name="adaln_mod",
    )(c, w_ada, b_ada.reshape(1, 3 * D_MODEL), g_norm.reshape(1, D_MODEL))


def _scan_rows(x, op, identity):
    n = x.shape[0]
    row = lax.broadcasted_iota(jnp.int32, x.shape, 0)
    sh = 1
    while sh < n:
        x = op(x, jnp.where(row >= sh, pltpu.roll(x, sh, 0), identity))
        sh *= 2
    return x


def _bf16_parts(x):
    hi = x.astype(BF16).astype(F32)
    mid = (x - hi).astype(BF16).astype(F32)
    return [hi, mid, x - hi - mid]


_N_PARTS = 3
_CUMSUM_ROWS = 4 * SUBLANES
_GATE_STACK_ROWS = 8 * SUBLANES


def _mlstm_kernel(x_ref, a_ref, shift_ref, wm_in, wgr_ref, bgr_ref, tri_ref, gsel_ref, p4_ref,
                  p16_ref, wconv_ref, bconv_ref, wq_ref, wk_ref, ghead_ref, skip_ref, wga_in,
                  wout_in, o_ref, hnat_ref, hd4_ref, hd16_ref, c_scr, n_scr, m_scr, xa_scr, hm_scr,
                  wm_ref, wga_ref, wout_ref):
    L = M_CHUNK
    W = M_WIDTH
    TAIL = SUBLANES

    @pl.when(pl.program_id(1) == 0)
    def _():
        c_scr[...] = jnp.zeros_like(c_scr)
        n_scr[...] = jnp.zeros_like(n_scr)
        m_scr[...] = jnp.zeros_like(m_scr)
        xa_scr[:, 0:TAIL, :] = jnp.zeros((M_BPS, TAIL, W), F32)

    @pl.when((pl.program_id(0) == 0) & (pl.program_id(1) == 0))
    def _():
        wm_ref[...] = wm_in[...]
        wga_ref[...] = wga_in[...]
        wout_ref[...] = wout_in[...]

    causal = (lax.broadcasted_iota(jnp.int32, (L, L), 0) >=
              lax.broadcasted_iota(jnp.int32, (L, L), 1))

    for bi in range(M_BPS):
        hnat_ref[bi] = _mod_norm(x_ref[bi], a_ref[bi], shift_ref[bi]).astype(BF16)
    hb_all = hnat_ref[...].reshape(M_BPS * L, D_MODEL)
    shared = {}

    def proj(name, w_cols):
        if name not in shared:
            shared[name] = _dot(hb_all, w_cols())
        return shared[name]

    def batch_stages(bi):
        rows = slice(bi * L, (bi + 1) * L)
        hb = hnat_ref[bi]

        def split_to(p_ref, ref):
            d = ref.shape[1]
            perm = _dot(p_ref[...], hb).astype(BF16)
            for r in range(d):
                ref[bi, r] = perm[r * (L // d):(r + 1) * (L // d), :]

        xa_scr[bi, TAIL:TAIL + L, :] = proj("xa", lambda: wm_ref[:, 0:W])[rows, :]
        gr = _dot_nt(wgr_ref[...], hb) + bgr_ref[...]
        yield
        split_to(p4_ref, hd4_ref)
        lf = _bf16_parts(_log_sigmoid(gr[SUBLANES:2 * SUBLANES, :]))
        lf.append(jnp.zeros((_CUMSUM_ROWS - _N_PARTS * SUBLANES, L), F32))
        csum = _dot(jnp.concatenate(lf, axis=0).astype(BF16), tri_ref[...])
        yield
        split_to(p16_ref, hd16_ref)
        a_row = csum[0:SUBLANES] + csum[SUBLANES:2 * SUBLANES] + csum[2 * SUBLANES:3 * SUBLANES]
        r_row = gr[0:SUBLANES, :] - a_row
        stack = _bf16_parts(r_row) + _bf16_parts(a_row)
        stack.append(jnp.zeros((_GATE_STACK_ROWS - 2 * _N_PARTS * SUBLANES, L), F32))
        rc_ac = _dot_tn(jnp.concatenate(stack, axis=0).astype(BF16), gsel_ref[...])
        ga = proj("ga", lambda: wga_ref[...])[rows, :]
        yield

        r_col, a_col = rc_ac[:, 0:LANES], rc_ac[:, LANES:2 * LANES]
        m_prev = m_scr[bi, 0:1, :]
        big_m = jnp.maximum(_scan_rows(r_col, jnp.maximum, -jnp.inf), m_prev)
        m_last = big_m[L - 1:L, :]
        w_inter_all = jnp.exp(m_prev - big_m)
        floor_all = jnp.exp(-a_col - big_m)
        kscale_all = jnp.exp(r_col - m_last)
        decay_all = jnp.exp(m_prev - m_last)
        m_scr[bi, 0:1, :] = a_col[L - 1:L, :] + m_last
        yield

        def head_stages(hd):
            cs = slice(hd * M_HDIM, (hd + 1) * M_HDIM)
            hc = slice(hd, hd + 1)
            conv = bconv_ref[:, cs]
            for j in range(CONV_W):
                off = TAIL - (CONV_W - 1) + j
                conv = conv + wconv_ref[j:j + 1, cs] * xa_scr[bi, off:off + L, cs]
            xc_h = _silu(conv)
            v_all = proj(("v", hd), lambda: wm_ref[:, W + hd * M_HDIM:W + (hd + 1) * M_HDIM])
            vb = v_all[rows, :].astype(BF16)
            yield

            xcb = xc_h.astype(BF16)
            qf = _dot(xcb, wq_ref[hd])
            qb = qf.astype(BF16)
            kf = _dot(xcb, wk_ref[hd])
            kb = kf.astype(BF16)
            w_intra = jnp.exp(jnp.where(causal, r_row[hc, :] - big_m[:, hc], -jnp.inf))
            yield

            s = _dot_nt(qb, kb) * w_intra
            yield

            w_inter = w_inter_all[:, hc]
            c_old = c_scr[bi, hd]
            n_old = n_scr[bi, hc, :]
            num = _dot(s.astype(BF16), vb) + w_inter * _dot(qb, c_old.astype(BF16))
            den = (jnp.sum(s, axis=1, keepdims=True) +
                   w_inter * jnp.sum(qf * n_old, axis=1, keepdims=True))
            hout = num / jnp.maximum(jnp.abs(den), floor_all[:, hc])
            yield

            kw = kscale_all[:, hc] * kf
            decay = decay_all[:, hc]
            c_scr[bi, hd] = decay * c_old + _dot_tn(kw.astype(BF16), vb)
            n_scr[bi, hc, :] = decay * n_old + jnp.sum(kw, axis=0, keepdims=True)
            yield

            oa = proj(("o", hd), lambda: wm_ref[:, 2 * W + hd * M_HDIM:2 * W + (hd + 1) * M_HDIM])
            hm = _sigmoid(oa[rows, :]) * hout
            hm = hm * lax.rsqrt(jnp.mean(hm * hm, axis=-1, keepdims=True) + EPS) * ghead_ref[:, cs]
            hm = hm + skip_ref[:, cs] * xc_h
            yield

            za = proj(("z", hd), lambda: wm_ref[:, 3 * W + hd * M_HDIM:3 * W + (hd + 1) * M_HDIM])
            hm_scr[rows, cs] = (hm * _silu(za[rows, :])).astype(BF16)

        yield from _staggered([head_stages(hd) for hd in range(M_HEADS)], H_LAG)

        xa_scr[bi, 0:TAIL, :] = xa_scr[bi, L:L + TAIL, :]
        y_m = _dot(hm_scr[rows, :], wout_ref[...])
        o_ref[bi] = (_sigmoid(ga) * y_m).astype(o_ref.dtype)

    _run_staggered([batch_stages(bi) for bi in range(M_BPS)], M_LAG)


def _mlstm_call(x, a_mod, shift_mod, wm, wgr, bgr, tri, gsel, p4, p16, wconv, bconv, wq, wk, ghead, skip,
                wga, wout):
    B, S, D = x.shape
    L, nb = M_CHUNK, M_BPS
    d4, d16 = ATTN_GROUPS[1][1], ATTN_GROUPS[2][1]
    assert S % L == 0 and B % nb == 0 and L % (d16 * 2 * SUBLANES) == 0
    tok = pl.BlockSpec((nb, L, D), lambda b, c: (b, c, 0))
    per_batch = pl.BlockSpec((nb, 1, D), lambda b, c: (b, 0, 0))
    split = lambda d: pl.BlockSpec((nb, d, L // d, D), lambda b, c: (b, 0, c, 0))
    args = (wm, wgr, bgr, tri, gsel, p4, p16, wconv, bconv, wq, wk, ghead, skip, wga, wout)
    return pl.pallas_call(
        _mlstm_kernel,
        grid=(B // nb, S // L),
        in_specs=[tok, per_batch, per_batch] + [_resident(a.shape) for a in args],
        out_specs=[tok, tok, split(d4), split(d16)],
        out_shape=[
            jax.ShapeDtypeStruct((B, S, D), BF16),
            jax.ShapeDtypeStruct((B, S, D), BF16),
            jax.ShapeDtypeStruct((B, d4, S // d4, D), BF16),
            jax.ShapeDtypeStruct((B, d16, S // d16, D), BF16),
        ],
        scratch_shapes=[
            pltpu.VMEM((nb, M_HEADS, M_HDIM, M_HDIM), F32),
            pltpu.VMEM((nb, SUBLANES, M_HDIM), F32),
            pltpu.VMEM((nb, SUBLANES, LANES), F32),
            pltpu.VMEM((nb, L + SUBLANES, M_WIDTH), F32),
            pltpu.VMEM((nb * L, M_WIDTH), BF16),
            pltpu.VMEM(wm.shape, BF16),
            pltpu.VMEM(wga.shape, BF16),
            pltpu.VMEM(wout.shape, BF16),
        ],
        compiler_params=pltpu.CompilerParams(dimension_semantics=("arbitrary", "arbitrary"),
                                             vmem_limit_bytes=VMEM_LIMIT),
        name="mlstm_branch",
    )(x, a_mod, shift_mod, *args)


def _attn_kernel(h_ref, pos_ref, freq_ref, esel_ref, w_in, o_ref, lse_ref, k_scr, v_scr, w_ref):
    n_res, n = h_ref.shape[0], h_ref.shape[1]
    ub = pl.program_id(2)

    @pl.when((pl.program_id(0) == 0) & (pl.program_id(1) == 0) & (ub == 0))
    def _():
        for blk in range(2 * A_HPG):
            cs_ = slice(blk * A_HDIM, (blk + 1) * A_HDIM)
            src = w_in[:, cs_]
            w_ref[:, cs_] = jnp.concatenate([src[:, lo:hi] for lo, hi in _ROPE_PIECES],
                                            axis=1).astype(BF16)
        w_ref[:, 2 * A_GW:3 * A_GW] = w_in[:, 2 * A_GW:3 * A_GW].astype(BF16)

    @pl.when(ub == 0)
    def _():
        k_scr[:, 0:A_SPAN, :] = jnp.zeros((n_res, A_SPAN, A_GW), BF16)
        v_scr[:, 0:A_SPAN, :] = jnp.zeros((n_res, A_SPAN, A_GW), BF16)

    @pl.when(ub > 0)
    def _():
        k_scr[:, 0:A_SPAN, :] = k_scr[:, n:n + A_SPAN, :]
        v_scr[:, 0:A_SPAN, :] = v_scr[:, n:n + A_SPAN, :]

    lane = lax.broadcasted_iota(jnp.int32, (1, LANES), 1)
    rot_lane = (lane % (LANES // 2)) < ROPE_HALF
    p_idx = lax.broadcasted_iota(jnp.int32, (A_SPAN, 2 * A_SPAN), 0)
    c_idx = lax.broadcasted_iota(jnp.int32, (A_SPAN, 2 * A_SPAN), 1)
    band = (c_idx >= p_idx) & (c_idx <= p_idx + A_SPAN)
    lane_o = lax.broadcasted_iota(jnp.int32, (A_SPAN, LANES), 1)
    scale = A_HDIM ** -0.5

    sub = min(A_SUB, n)

    def subtile_stages(rr, st):
        r0 = st * sub
        hb = h_ref[rr, r0:r0 + sub, :]

        ang = freq_ref[...] * pos_ref[rr, :, r0:r0 + sub].astype(F32)
        cs = jnp.concatenate([jnp.cos(ang), jnp.sin(ang)], axis=0)
        cs_hi = cs.astype(BF16)
        cs_lo = (cs - cs_hi.astype(F32)).astype(BF16)
        tab = _dot_tn(jnp.concatenate([cs_hi, cs_lo], axis=0), esel_ref[...])
        q = _dot(hb, w_ref[:, 0:A_GW])
        k = _dot(hb, w_ref[:, A_GW:2 * A_GW])
        yield

        cos_t = tab[:, 0:LANES] + jnp.where(rot_lane, 0.0, 1.0)
        sin_t = tab[:, LANES:2 * LANES]

        def rope(t):
            return t * cos_t + pltpu.roll(t, LANES // 2, 1) * sin_t

        qs = []
        for hd in range(A_HPG):
            cs_ = slice(hd * A_HDIM, (hd + 1) * A_HDIM)
            qs.append((rope(q[:, cs_]) * scale).astype(BF16))
            k_scr[rr, A_SPAN + r0:A_SPAN + r0 + sub, cs_] = rope(k[:, cs_]).astype(BF16)
        v_scr[rr, A_SPAN + r0:A_SPAN + r0 + sub, :] = (
            _dot(hb, w_ref[:, 2 * A_GW:3 * A_GW]).astype(BF16))
        yield

        first_valid = band & (c_idx >= jnp.where(ub == 0, A_SPAN, 0))
        lses = {}

        def unit_stages(jb, hd):
            rows = slice(r0 + jb * A_SPAN, r0 + (jb + 1) * A_SPAN)
            krows = slice(r0 + jb * A_SPAN, r0 + (jb + 2) * A_SPAN)
            cs_ = slice(hd * A_HDIM, (hd + 1) * A_HDIM)
            s = _dot_nt(qs[hd][jb * A_SPAN:(jb + 1) * A_SPAN, :], k_scr[rr, krows, cs_])
            yield
            s = jnp.where(first_valid if (st == 0 and jb == 0) else band, s, -jnp.inf)
            mx = jnp.max(s, axis=1, keepdims=True)
            yield
            e = jnp.exp(s - mx)
            den = jnp.sum(e, axis=1, keepdims=True)
            eb = e.astype(BF16)
            yield
            o = _dot(eb, v_scr[rr, krows, cs_])
            yield
            o = o / den
            o_ref[rr, rows, cs_] = o.astype(o_ref.dtype)
            lses[jb, hd] = mx + jnp.log(den)
            if hd == A_HPG - 1:
                lse_blk = jnp.zeros((A_SPAN, LANES), F32)
                for h2 in range(A_HPG):
                    lse_blk = jnp.where(lane_o == h2, lses[jb, h2], lse_blk)
                lse_ref[rr, rows, :] = lse_blk

        yield from _staggered([unit_stages(jb, hd) for jb in range(sub // A_SPAN)
                               for hd in range(A_HPG)], U_LAG)

    _run_staggered([subtile_stages(rr, st) for rr in range(n_res) for st in range(n // sub)], A_LAG)


def _attn_call(h_split, positions, freq, esel, w_qkv):
    B, d, U, D = h_split.shape
    n = min(A_TILE, U)
    n_res = min(d, A_TILE // n)
    assert U % n == 0 and n % min(A_SUB, n) == 0 and A_SUB % A_SPAN == 0 and d % n_res == 0
    pos = positions.reshape(B, U, d).transpose(0, 2, 1).reshape(B, d, 1, U)
    tok = lambda w: pl.BlockSpec((None, n_res, n, w), lambda b, r, u: (b, r, u, 0))
    return pl.pallas_call(
        _attn_kernel,
        grid=(B, d // n_res, U // n),
        in_specs=[
            tok(D),
            pl.BlockSpec((None, n_res, 1, n), lambda b, r, u: (b, r, 0, u)),
            _resident(freq.shape), _resident(esel.shape), _resident(w_qkv.shape),
        ],
        out_specs=[tok(A_GW), tok(LANES)],
        out_shape=[
            jax.ShapeDtypeStruct((B, d, U, A_GW), BF16),
            jax.ShapeDtypeStruct((B, d, U, LANES), F32),
        ],
        scratch_shapes=[
            pltpu.VMEM((n_res, A_SPAN + n, A_GW), BF16),
            pltpu.VMEM((n_res, A_SPAN + n, A_GW), BF16),
            pltpu.VMEM(w_qkv.shape, BF16),
        ],
        compiler_params=pltpu.CompilerParams(
            dimension_semantics=("arbitrary", "arbitrary", "arbitrary"),
            vmem_limit_bytes=VMEM_LIMIT),
        name=f"dilated_attn_d{d}",
    )(h_split, pos, freq, esel, w_qkv)


def _final_kernel(x_ref, h_ref, gate_ref, ym_ref, o1_ref, o2_ref, o3_ref, l1_ref, l2_ref, l3_ref,
                  wz_in, wgb_in, woa_in, wout_in, gfin_ref, out_ref, oz_scr, o_scr, l_scr,
                  wz_ref, wgb_ref, woa_ref, wout_ref):
    T = F_TILE

    @pl.when((pl.program_id(0) == 0) & (pl.program_id(1) == 0))
    def _():
        wz_ref[...] = wz_in[...]
        wgb_ref[...] = wgb_in[...]
        woa_ref[...] = woa_in[...]
        wout_ref[...] = wout_in[...]

    subs = [slice(i * F_SUB, (i + 1) * F_SUB) for i in range(T // F_SUB)]

    zbs = [_dot(h_ref[rs, :], wz_ref[...]) for rs in subs]
    gbs = [_dot(h_ref[rs, :], wgb_ref[...]) for rs in subs]

    for gi, (o_ref, l_ref) in enumerate(((o1_ref, l1_ref), (o2_ref, l2_ref), (o3_ref, l3_ref))):
        d = ATTN_GROUPS[gi][1]
        for r in range(d):
            rows = pl.ds(r, T // d, stride=d)
            l_scr[gi, rows, :] = l_ref[r]
            for hd in range(A_HPG):
                o_scr[gi * A_HPG + hd, rows, :] = (
                    o_ref[r, :, hd * A_HDIM:(hd + 1) * A_HDIM].astype(F32))

    y_as = []
    for rs, zb in zip(subs, zbs):
        l1, l2, l3 = l_scr[0, rs, :], l_scr[1, rs, :], l_scr[2, rs, :]
        mx = jnp.maximum(jnp.maximum(l1, l2), l3)
        e1, e2, e3 = jnp.exp(l1 - mx), jnp.exp(l2 - mx), jnp.exp(l3 - mx)
        inv = 1.0 / (e1 + e2 + e3)
        w1, w2, w3 = e1 * inv, e2 * inv, e3 * inv
        for hd in range(A_HPG):
            cs = slice(hd * A_HDIM, (hd + 1) * A_HDIM)
            hc = slice(hd, hd + 1)
            o_a = (w1[:, hc] * o_scr[hd, rs, :] + w2[:, hc] * o_scr[A_HPG + hd, rs, :] +
                   w3[:, hc] * o_scr[2 * A_HPG + hd, rs, :])
            oz_scr[rs, cs] = (o_a * _silu(zb[:, cs])).astype(BF16)
        y_as.append(_dot(oz_scr[rs, :], woa_ref[...]))

    deltas = []
    for rs, gb, y_a in zip(subs, gbs, y_as):
        merged = ym_ref[rs, :].astype(F32) + _sigmoid(gb) * y_a
        deltas.append(_dot(merged.astype(BF16), wout_ref[...]))

    for rs, delta in zip(subs, deltas):
        xn = x_ref[rs, :] + gate_ref[...] * delta
        out_ref[rs, :] = (xn * lax.rsqrt(jnp.mean(xn * xn, axis=-1, keepdims=True) + EPS) *
                          gfin_ref[...])


def _final_call(x, h_nat, gate_mod, ym, os_, ls_, wz, wgb, woa, wout, gfin):
    B, S, D = x.shape
    T = F_TILE
    assert S % T == 0 and all(T % (d * 2 * SUBLANES) == 0 for _, d in ATTN_GROUPS)
    tok = lambda w: pl.BlockSpec((None, T, w), lambda b, t: (b, t, 0))
    split = lambda a: pl.BlockSpec((None, a.shape[1], T // a.shape[1], a.shape[3]),
                                   lambda b, t: (b, 0, t, 0))
    per_batch = pl.BlockSpec((None, 1, D), lambda b, t: (b, 0, 0))
    wts = (wz, wgb, woa, wout, gfin)
    return pl.pallas_call(
        _final_kernel,
        grid=(B, S // T),
        in_specs=([tok(D), tok(D), per_batch, tok(D)] + [split(a) for a in os_] +
                  [split(a) for a in ls_] + [_resident(a.shape) for a in wts]),
        out_specs=tok(D),
        out_shape=jax.ShapeDtypeStruct((B, S, D), F32),
        scratch_shapes=[
            pltpu.VMEM((T, A_GW), BF16),
            pltpu.VMEM((N_GROUPS * A_HPG, T, A_HDIM), F32),
            pltpu.VMEM((N_GROUPS, T, LANES), F32),
            pltpu.VMEM(wz.shape, BF16),
            pltpu.VMEM(wgb.shape, BF16),
            pltpu.VMEM(woa.shape, BF16),
            pltpu.VMEM(wout.shape, BF16),
        ],
        compiler_params=pltpu.CompilerParams(dimension_semantics=("arbitrary", "arbitrary"),
                                             vmem_limit_bytes=VMEM_LIMIT),
        name="merge_out_norm",
    )(x, h_nat, gate_mod, ym, *os_, *ls_, *wts)


def _rope_tables():
    inv_freq = ROPE_THETA ** (-jnp.arange(ROPE_HALF, dtype=F32) / ROPE_HALF)
    e = np.zeros((4 * ROPE_HALF, 2 * LANES), np.float32)
    for part in range(2):
        for i in range(ROPE_HALF):
            rc, rs = part * 2 * ROPE_HALF + i, part * 2 * ROPE_HALF + ROPE_HALF + i
            e[rc, i] = 1.0
            e[rc, LANES // 2 + i] = 1.0
            e[rs, LANES + i] = -1.0
            e[rs, LANES + LANES // 2 + i] = 1.0
    return inv_freq.reshape(ROPE_HALF, 1), jnp.asarray(e, BF16)


def _gate_select():
    e = np.zeros((_GATE_STACK_ROWS, 2 * LANES), np.float32)
    for q in range(2):
        for part in range(_N_PARTS):
            for hd in range(M_HEADS):
                e[(q * _N_PARTS + part) * SUBLANES + hd, q * LANES + hd] = 1.0
    return jnp.asarray(e, BF16)


def _cumsum_triangle():
    return jnp.asarray(np.triu(np.ones((M_CHUNK, M_CHUNK), np.float32)), BF16)


def _split_perm(d):
    p = np.zeros((M_CHUNK, M_CHUNK), np.float32)
    out = np.arange(M_CHUNK)
    p[out, (out % (M_CHUNK // d)) * d + out // (M_CHUNK // d)] = 1.0
    return jnp.asarray(p, BF16)


def kernel(x, c, positions, w_ada, b_ada, g_norm, w_in, b_igate, b_fgate, w_conv, b_conv,
           w_q_mlstm, w_k_mlstm, g_mlstm_head, skip_mlstm, w_out_mlstm, w_out_attn, w_out, g_final):
    B, S, D = x.shape
    assert D == D_MODEL and w_ada.shape[0] == 1
    W = M_WIDTH
    QKV = N_GROUPS * A_GW
    o_gate = 4 * W
    o_q = o_gate + 2 * M_HEADS
    o_k, o_v = o_q + QKV, o_q + 2 * QKV
    o_zb = o_q + 3 * QKV
    o_ga = o_zb + A_GW
    o_gb = o_ga + D
    w = w_in[0]

    mod = _mod_call(c, w_ada[0], b_ada[0], g_norm[0])
    shift_mod = mod[0].reshape(B, 1, D)
    a_mod = mod[1].reshape(B, 1, D)
    gate_mod = mod[2].reshape(B, 1, D)

    w_i = w[:, o_gate:o_gate + M_HEADS]
    w_f = w[:, o_gate + M_HEADS:o_gate + 2 * M_HEADS]
    pad_r = jnp.zeros((SUBLANES - M_HEADS, D), F32)
    wgr = jnp.concatenate([w_i.T, pad_r, w_f.T, pad_r], axis=0).astype(BF16)
    zr = jnp.zeros((SUBLANES - M_HEADS,), F32)
    bgr = jnp.concatenate([b_igate[0], zr, b_fgate[0], zr]).reshape(2 * SUBLANES, 1)
    ym, h_nat, h_d4, h_d16 = _mlstm_call(
        x, a_mod, shift_mod, w[:, 0:4 * W].astype(BF16), wgr, bgr, _cumsum_triangle(), _gate_select(),
        _split_perm(ATTN_GROUPS[1][1]), _split_perm(ATTN_GROUPS[2][1]),
        w_conv[0], b_conv[0].reshape(1, W), w_q_mlstm[0].astype(BF16),
        (w_k_mlstm[0] * (M_HDIM ** -0.5)).astype(BF16),
        g_mlstm_head[0].reshape(1, W), skip_mlstm[0].reshape(1, W),
        w[:, o_ga:o_ga + D].astype(BF16), w_out_mlstm[0].astype(BF16))

    freq, esel = _rope_tables()
    os_, ls_ = [], []
    for gi, h_split in enumerate((h_nat.reshape(B, 1, S, D), h_d4, h_d16)):
        assert h_split.shape[1] == ATTN_GROUPS[gi][1]
        gs = slice(gi * A_GW, (gi + 1) * A_GW)
        w_qkv = jnp.concatenate([w[:, o_q:o_k][:, gs], w[:, o_k:o_v][:, gs], w[:, o_v:o_zb][:, gs]],
                                axis=1)
        o_g, l_g = _attn_call(h_split, positions, freq, esel, w_qkv)
        os_.append(o_g)
        ls_.append(l_g)

    return _final_call(
        x, h_nat, gate_mod, ym, os_, ls_,
        w[:, o_zb:o_zb + A_GW].astype(BF16), w[:, o_gb:o_gb + D].astype(BF16),
        w_out_attn[0].astype(BF16), w_out[0].astype(BF16), g_final.reshape(1, D))
```
